```python
import math
import jax, jax.numpy as jnp
from jax import lax
import numpy as np

D_MODEL = 1024
BATCH = 4
SEQ = 4096
DEPTH = 4
DEC_BATCH = 32
DEC_SEQ = 8
PAST_LEN = 8192
PAGE_SIZE = 128

N_MIXERS = 4
N_SUB = 3
D_FF = 2816
NORM_EPS = 1e-6
HEAD_DIM = 64
ATTN_SCALE = HEAD_DIM ** -0.5
QBLK = 128
NEG_INF = -1e30
CONV_A = 31
B_PAIRS = ((128, 1), (512, 4), (2048, 16))
B_GROUPS = len(B_PAIRS)
B_HEADS = 16
C_HEADS = 16
FORGET_BIAS = 3.0
CONV_D = 3
REL_BUCKETS = 32
REL_MAX_DIST = 2048
N_A = len(range(0, DEPTH, N_MIXERS))
N_B = len(range(1, DEPTH, N_MIXERS))
N_C = len(range(2, DEPTH, N_MIXERS))
N_D = len(range(3, DEPTH, N_MIXERS))

kernel_name = 'hybrid_conformer_longnet_fox_shortconv_step'


def rmsnorm(x, g):
    xf = x.astype(jnp.float32)
    y = xf * lax.rsqrt(jnp.mean(xf * xf, axis=-1, keepdims=True) + NORM_EPS)
    return (y * g.astype(jnp.float32)).astype(x.dtype)


def layernorm(x, g, b):
    xf = x.astype(jnp.float32)
    mu = jnp.mean(xf, axis=-1, keepdims=True)
    var = jnp.mean(jnp.square(xf - mu), axis=-1, keepdims=True)
    y = (xf - mu) * lax.rsqrt(var + NORM_EPS) * g.astype(jnp.float32) + b.astype(jnp.float32)
    return y.astype(x.dtype)


def swiglu(h, w13, w2):
    g, up = jnp.split(h @ w13, 2, axis=-1)
    return (jax.nn.silu(g) * up) @ w2


def t5_bucket(dist):
    max_exact = REL_BUCKETS // 2
    df = jnp.maximum(dist, 1).astype(jnp.float32)
    large = max_exact + (jnp.log(df / max_exact) / math.log(REL_MAX_DIST / max_exact)
                         * (REL_BUCKETS - max_exact)).astype(jnp.int32)
    large = jnp.minimum(large, REL_BUCKETS - 1)
    return jnp.where(dist < max_exact, dist, large)


def causal_dwconv(u, buf, w):
    k = w.shape[0]
    cat = jnp.concatenate([buf.astype(u.dtype), u], axis=1)
    y = lax.conv_general_dilated(cat, w[:, None, :].astype(u.dtype), window_strides=(1,), padding='VALID',
                                 dimension_numbers=('NWC', 'WIO', 'NWC'), feature_group_count=u.shape[-1])
    return y, cat[:, cat.shape[1] - (k - 1):]


def conformer_conv(h, buf, w_in, b_in, w_dw, b_dw, ln_g, ln_b, w_out, b_out):
    g = h @ w_in + b_in
    u = g[..., :D_MODEL] * jax.nn.sigmoid(g[..., D_MODEL:])
    y, new_buf = causal_dwconv(u, buf, w_dw)
    y = jax.nn.silu(layernorm(y + b_dw, ln_g, ln_b))
    return y @ w_out + b_out, new_buf


def short_conv(h, buf, w_in, w_conv, w_out):
    bg, cg, hv = jnp.split(h @ w_in, 3, axis=-1)
    y, new_buf = causal_dwconv(cg * hv, buf, w_conv)
    return (bg * y) @ w_out, new_buf


def dilated_project(h, w_qkv):
    bn, t = h.shape[:2]
    qkv = (h @ w_qkv).reshape(bn, t, 3, B_GROUPS, B_HEADS, HEAD_DIM)
    return qkv[:, :, 0], qkv[:, :, 1], qkv[:, :, 2]


def dilated_prompt(q, k, v, dil, ns, bias_tab):
    bn, s_len, nh, dh = q.shape
    L = s_len // dil
    z = bn * dil

    def fold(t):
        return t.reshape(bn, L, dil, nh, dh).transpose(0, 2, 1, 3, 4).reshape(z, L, nh, dh)

    qf, kf, vf = fold(q), fold(k), fold(v)
    bq = min(QBLK, L)
    nblk = -(-L // bq)
    lp = nblk * bq
    qb = jnp.pad(qf, ((0, 0), (0, lp - L), (0, 0), (0, 0))).reshape(z, nblk, bq, nh, dh)
    pad_kv = ((0, 0), (ns, lp - L), (0, 0), (0, 0))
    idx = jnp.arange(nblk)[:, None] * bq + jnp.arange(bq + ns)[None, :]
    kw = jnp.pad(kf, pad_kv)[:, idx]
    vw = jnp.pad(vf, pad_kv)[:, idx]
    steps = jnp.arange(bq)[:, None] - jnp.arange(bq + ns)[None, :] + ns
    valid = ((steps >= 0) & (steps <= ns))[None] & ((idx - ns) >= 0)[:, None, :]
    bias = bias_tab[t5_bucket(jnp.clip(steps, 0, ns) * dil)].transpose(2, 0, 1).astype(jnp.float32)
    s = jnp.einsum('znqhd,znkhd->znhqk', qb, kw).astype(jnp.float32) * ATTN_SCALE + bias
    s = jnp.where(valid[None, :, None], s, NEG_INF)
    lse = jax.nn.logsumexp(s, axis=-1)
    p = jnp.exp(s - lse[..., None]).astype(v.dtype)
    o = jnp.einsum('znhqk,znkhd->znqhd', p, vw).reshape(z, lp, nh, dh)[:, :L]
    lse = lse.transpose(0, 1, 3, 2).reshape(z, lp, nh)[:, :L]
    o = o.reshape(bn, dil, L, nh, dh).transpose(0, 2, 1, 3, 4).reshape(bn, s_len, nh, dh)
    lse = lse.reshape(bn, dil, L, nh).transpose(0, 2, 1, 3).reshape(bn, s_len, nh)
    return o, lse


def dilated_sample(q, k, v, buf, dil, ns, bias_tab):
    t = q.shape[1]
    lb = buf.shape[1]
    ck = jnp.concatenate([buf[:, :, 0].astype(k.dtype), k], axis=1)
    cv = jnp.concatenate([buf[:, :, 1].astype(v.dtype), v], axis=1)
    j = jnp.arange(ns + 1)
    idx = lb + jnp.arange(t)[:, None] - j[None, :] * dil
    valid = idx >= 0
    idxc = jnp.maximum(idx, 0)
    bias = bias_tab[t5_bucket(j * dil)].T.astype(jnp.float32)
    s = jnp.einsum('bthd,btjhd->bhtj', q, ck[:, idxc]).astype(jnp.float32) * ATTN_SCALE + bias[None, :, None, :]
    s = jnp.where(valid[None, None], s, NEG_INF)
    lse = jax.nn.logsumexp(s, axis=-1)
    p = jnp.exp(s - lse[..., None]).astype(v.dtype)
    o = jnp.einsum('bhtj,btjhd->bthd', p, cv[:, idxc])
    new_buf = jnp.stack([ck[:, t:], cv[:, t:]], axis=2)
    return o, lse.transpose(0, 2, 1), new_buf


def merge_groups(outs, lses, w_out):
    wts = jax.nn.softmax(jnp.stack(lses), axis=0)
    o = jnp.sum(wts[..., None].astype(outs[0].dtype) * jnp.stack(outs), axis=0)
    bn, t = o.shape[:2]
    return o.reshape(bn, t, B_HEADS * HEAD_DIM) @ w_out


def dilated_mixer_prompt(h, w_qkv, w_out, rel_bias):
    q, k, v = dilated_project(h, w_qkv)
    s_len = h.shape[1]
    outs, lses, bufs = [], [], []
    for g, (win, dil) in enumerate(B_PAIRS):
        o, lse = dilated_prompt(q[:, :, g], k[:, :, g], v[:, :, g], dil, win // dil,
                                rel_bias[:, g * B_HEADS:(g + 1) * B_HEADS])
        outs.append(o)
        lses.append(lse)
        lw = min(win, s_len)
        bufs.append(jnp.stack([k[:, s_len - lw:, g], v[:, s_len - lw:, g]], axis=2))
    return merge_groups(outs, lses, w_out), tuple(bufs)


def dilated_mixer_sample(h, bufs, w_qkv, w_out, rel_bias):
    q, k, v = dilated_project(h, w_qkv)
    outs, lses, new_bufs = [], [], []
    for g, (win, dil) in enumerate(B_PAIRS):
        o, lse, nb = dilated_sample(q[:, :, g], k[:, :, g], v[:, :, g], bufs[g], dil, win // dil,
                                    rel_bias[:, g * B_HEADS:(g + 1) * B_HEADS])
        outs.append(o)
        lses.append(lse)
        new_bufs.append(nb)
    return merge_groups(outs, lses, w_out), tuple(new_bufs)


def fox_project(h, w_in, b_f):
    bn, t = h.shape[:2]
    p = h @ w_in
    qkv = p[..., :3 * C_HEADS * HEAD_DIM].reshape(bn, t, 3, C_HEADS, HEAD_DIM)
    logf = jax.nn.log_sigmoid((p[..., 3 * C_HEADS * HEAD_DIM:] + b_f).astype(jnp.float32))
    return qkv[:, :, 0], qkv[:, :, 1], qkv[:, :, 2], logf


def fox_mixer_prompt(h, w_in, b_f, w_out):
    q, k, v, logf = fox_project(h, w_in, b_f)
    bn, s_len = h.shape[:2]
    cum_t = jnp.cumsum(logf, axis=1).transpose(0, 2, 1)
    kpos = jnp.arange(s_len)

    def block(n):
        st = n * QBLK
        qb = lax.dynamic_slice_in_dim(q, st, QBLK, axis=1)
        cq = lax.dynamic_slice_in_dim(cum_t, st, QBLK, axis=2)
        s = (jnp.einsum('bqhd,bkhd->bhqk', qb, k).astype(jnp.float32) * ATTN_SCALE
             + cq[..., None] - cum_t[:, :, None, :])
        causal = (st + jnp.arange(QBLK))[:, None] >= kpos[None, :]
        p = jax.nn.softmax(jnp.where(causal, s, NEG_INF), axis=-1).astype(v.dtype)
        return jnp.einsum('bhqk,bkhd->bqhd', p, v)

    o = lax.map(block, jnp.arange(s_len // QBLK))
    o = o.transpose(1, 0, 2, 3, 4).reshape(bn, s_len, C_HEADS * HEAD_DIM)
    n_pg = s_len // PAGE_SIZE
    kv_pages = jnp.stack([k, v], axis=2).reshape(bn, n_pg, PAGE_SIZE, 2, C_HEADS, HEAD_DIM)
    logf_pages = logf.reshape(bn, n_pg, PAGE_SIZE, C_HEADS)
    return o @ w_out, (kv_pages, logf_pages)


def fox_mixer_sample(h, cache_kv, cache_logf, u, page_table, w_in, b_f, w_out):
    q, k, v, logf = fox_project(h, w_in, b_f)
    bn, t = h.shape[:2]
    past = page_table.shape[1] * cache_kv.shape[2]
    k_past = cache_kv[u, page_table, :, 0].reshape(bn, past, C_HEADS, HEAD_DIM).astype(k.dtype)
    v_past = cache_kv[u, page_table, :, 1].reshape(bn, past, C_HEADS, HEAD_DIM).astype(v.dtype)
    lf_past = cache_logf[u, page_table].reshape(bn, past, C_HEADS).astype(jnp.float32)
    cum = jnp.cumsum(jnp.concatenate([lf_past, logf], axis=1), axis=1).transpose(0, 2, 1)
    cq = cum[:, :, past:]
    s_past = (jnp.einsum('bthd,bphd->bhtp', q, k_past).astype(jnp.float32) * ATTN_SCALE
              + cq[..., None] - cum[:, :, None, :past])
    s_new = (jnp.einsum('bthd,bshd->bhts', q, k).astype(jnp.float32) * ATTN_SCALE
             + cq[..., None] - cq[:, :, None, :])
    causal = jnp.arange(t)[:, None] >= jnp.arange(t)[None, :]
    s_new = jnp.where(causal, s_new, NEG_INF)
    p = jax.nn.softmax(jnp.concatenate([s_past, s_new], axis=-1), axis=-1).astype(v.dtype)
    o = (jnp.einsum('bhtp,bphd->bthd', p[..., :past], v_past)
         + jnp.einsum('bhts,bshd->bthd', p[..., past:], v))
    y = o.reshape(bn, t, C_HEADS * HEAD_DIM) @ w_out
    return y, (jnp.stack([k, v], axis=2), logf)


def trunk_layer(x, c, l, mixer, ada_w, ada_b, norm_g, ffn_w13, ffn_w2):
    bn = c.shape[0]
    mod = (jax.nn.silu(c) @ ada_w[l] + ada_b[l]).reshape(bn, 3 * N_SUB, D_MODEL).astype(x.dtype)

    def branch(x, j, f, weight):
        h = rmsnorm(x, norm_g[l, j]) * (1 + mod[:, 3 * j + 1, None]) + mod[:, 3 * j, None]
        y, st = f(h)
        return x + weight * mod[:, 3 * j + 2, None] * y, st

    x, _ = branch(x, 0, lambda h: (swiglu(h, ffn_w13[l, 0], ffn_w2[l, 0]), None), 0.5)
    x, st = branch(x, 1, mixer, 1.0)
    x, _ = branch(x, 2, lambda h: (swiglu(h, ffn_w13[l, 1], ffn_w2[l, 1]), None), 0.5)
    return x, st


def setup_inputs(seed: int = 0) -> dict:
    key = jax.random.key(seed)
    ks = iter(jax.random.split(key, 64))

    def nrm(shape, scale=1.0):
        return jax.random.normal(next(ks), shape, jnp.float32) * scale

    D = D_MODEL
    n_pages = PAST_LEN // PAGE_SIZE
    n_used = DEC_BATCH * n_pages
    n_phys = n_used + max(1, n_used // 4)
    hd_b = B_HEADS * HEAD_DIM
    hd_c = C_HEADS * HEAD_DIM
    ada_base = jnp.tile(jnp.concatenate([jnp.zeros((2 * D,), jnp.float32), jnp.ones((D,), jnp.float32)]), N_SUB)
    return {
        'x_prompt': nrm((BATCH, SEQ, D)),
        'x_sample': nrm((DEC_BATCH, DEC_SEQ, D)),
        'c_prompt': nrm((BATCH, D)),
        'c_sample': nrm((DEC_BATCH, D)),
        'state_a_conv': nrm((N_A, DEC_BATCH, CONV_A - 1, D)),
        'state_b_kv_w128': nrm((N_B, DEC_BATCH, min(B_PAIRS[0][0], PAST_LEN), 2, B_HEADS, HEAD_DIM)),
        'state_b_kv_w512': nrm((N_B, DEC_BATCH, min(B_PAIRS[1][0], PAST_LEN), 2, B_HEADS, HEAD_DIM)),
        'state_b_kv_w2048': nrm((N_B, DEC_BATCH, min(B_PAIRS[2][0], PAST_LEN), 2, B_HEADS, HEAD_DIM)),
        'cache_c_kv': nrm((N_C, n_phys, PAGE_SIZE, 2, C_HEADS, HEAD_DIM)),
        'cache_c_logf': jax.nn.log_sigmoid(FORGET_BIAS + nrm((N_C, n_phys, PAGE_SIZE, C_HEADS))),
        'page_table': jax.random.permutation(next(ks), n_phys)[:n_used].reshape(DEC_BATCH, n_pages).astype(jnp.int32),
        'state_d_conv': nrm((N_D, DEC_BATCH, CONV_D - 1, D)),
        'ada_w': nrm((DEPTH, D, 3 * N_SUB * D), 0.3 * D ** -0.5),
        'ada_b': ada_base + nrm((DEPTH, 3 * N_SUB * D), 0.02),
        'norm_g': 1.0 + nrm((DEPTH, N_SUB, D), 0.02),
        'final_g': 1.0 + nrm((D,), 0.02),
        'ffn_w13': nrm((DEPTH, 2, D, 2 * D_FF), D ** -0.5),
        'ffn_w2': nrm((DEPTH, 2, D_FF, D), D_FF ** -0.5),
        'a_w_in': nrm((N_A, D, 2 * D), D ** -0.5),
        'a_b_in': nrm((N_A, 2 * D), 0.02),
        'a_w_dw': nrm((N_A, CONV_A, D), CONV_A ** -0.5),
        'a_b_dw': nrm((N_A, D), 0.02),
        'a_ln_g': 1.0 + nrm((N_A, D), 0.02),
        'a_ln_b': nrm((N_A, D), 0.02),
        'a_w_out': nrm((N_A, D, D), D ** -0.5),
        'a_b_out': nrm((N_A, D), 0.02),
        'b_w_qkv': nrm((N_B, D, 3 * B_GROUPS * hd_b), D ** -0.5),
        'b_w_out': nrm((N_B, hd_b, D), hd_b ** -0.5),
        'rel_bias': nrm((REL_BUCKETS, B_GROUPS * B_HEADS), 0.5),
        'c_w_in': nrm((N_C, D, 3 * hd_c + C_HEADS), D ** -0.5),
        'c_b_f': FORGET_BIAS + nrm((N_C, C_HEADS), 0.1),
        'c_w_out': nrm((N_C, hd_c, D), hd_c ** -0.5),
        'd_w_in': nrm((N_D, D, 3 * D), D ** -0.5),
        'd_w_conv': nrm((N_D, CONV_D, D), CONV_D ** -0.5),
        'd_w_out': nrm((N_D, D, D), D ** -0.5),
    }


def reference(x_prompt, x_sample, c_prompt, c_sample, state_a_conv, state_b_kv_w128, state_b_kv_w512,
              state_b_kv_w2048, cache_c_kv, cache_c_logf, page_table, state_d_conv, ada_w, ada_b, norm_g,
              final_g, ffn_w13, ffn_w2, a_w_in, a_b_in, a_w_dw, a_b_dw, a_ln_g, a_ln_b, a_w_out, a_b_out,
              b_w_qkv, b_w_out, rel_bias, c_w_in, c_b_f, c_w_out, d_w_in, d_w_conv, d_w_out):
    b_bufs = (state_b_kv_w128, state_b_kv_w512, state_b_kv_w2048)
    sp = [[] for _ in range(N_MIXERS)]
    ss = [[] for _ in range(N_MIXERS)]
    xp, xs = x_prompt, x_sample
    for l in range(DEPTH):
        kind, u = l % N_MIXERS, l // N_MIXERS
        if kind == 0:
            pa = (a_w_in[u], a_b_in[u], a_w_dw[u], a_b_dw[u], a_ln_g[u], a_ln_b[u], a_w_out[u], a_b_out[u])
            mix_p = lambda h: conformer_conv(h, jnp.zeros((h.shape[0], CONV_A - 1, D_MODEL), h.dtype), *pa)
            mix_s = lambda h: conformer_conv(h, state_a_conv[u], *pa)
        elif kind == 1:
            mix_p = lambda h: dilated_mixer_prompt(h, b_w_qkv[u], b_w_out[u], rel_bias)
            mix_s = lambda h: dilated_mixer_sample(h, (b_bufs[0][u], b_bufs[1][u], b_bufs[2][u]),
                                                   b_w_qkv[u], b_w_out[u], rel_bias)
        elif kind == 2:
            mix_p = lambda h: fox_mixer_prompt(h, c_w_in[u], c_b_f[u], c_w_out[u])
            mix_s = lambda h: fox_mixer_sample(h, cache_c_kv, cache_c_logf, u, page_table,
                                               c_w_in[u], c_b_f[u], c_w_out[u])
        else:
            mix_p = lambda h: short_conv(h, jnp.zeros((h.shape[0], CONV_D - 1, D_MODEL), h.dtype),
                                         d_w_in[u], d_w_conv[u], d_w_out[u])
            mix_s = lambda h: short_conv(h, state_d_conv[u], d_w_in[u], d_w_conv[u], d_w_out[u])
        xp, st_p = trunk_layer(xp, c_prompt, l, mix_p, ada_w, ada_b, norm_g, ffn_w13, ffn_w2)
        xs, st_s = trunk_layer(xs, c_sample, l, mix_s, ada_w, ada_b, norm_g, ffn_w13, ffn_w2)
        sp[kind].append(st_p)
        ss[kind].append(st_s)

    y_prompt = rmsnorm(xp, final_g)
    y_sample = rmsnorm(xs, final_g)
    new_a_conv_prompt = jnp.stack(sp[0])
    new_a_conv_sample = jnp.stack(ss[0])
    new_b_w128_prompt = jnp.stack([st[0] for st in sp[1]])
    new_b_w128_sample = jnp.stack([st[0] for st in ss[1]])
    new_b_w512_prompt = jnp.stack([st[1] for st in sp[1]])
    new_b_w512_sample = jnp.stack([st[1] for st in ss[1]])
    new_b_w2048_prompt = jnp.stack([st[2] for st in sp[1]])
    new_b_w2048_sample = jnp.stack([st[2] for st in ss[1]])
    new_c_kv_prompt = jnp.stack([st[0] for st in sp[2]])
    new_c_logf_prompt = jnp.stack([st[1] for st in sp[2]])
    new_c_kv_sample = jnp.stack([st[0] for st in ss[2]])
    new_c_logf_sample = jnp.stack([st[1] for st in ss[2]])
    new_d_conv_prompt = jnp.stack(sp[3])
    new_d_conv_sample = jnp.stack(ss[3])
    return (y_prompt, y_sample, new_a_conv_prompt, new_a_conv_sample, new_b_w128_prompt, new_b_w128_sample,
            new_b_w512_prompt, new_b_w512_sample, new_b_w2048_prompt, new_b_w2048_sample,
            new_c_kv_prompt, new_c_logf_prompt, new_c_kv_sample, new_c_logf_sample,
            new_d_conv_prompt, new_d_conv_sample)
```

```python
import functools
import math

import jax
import jax.numpy as jnp
import numpy as np
from jax import lax
from jax.experimental import pallas as pl
from jax.experimental.pallas import tpu as pltpu

F32 = jnp.float32
BF16 = jnp.bfloat16

D_MODEL = 1024
D_FF = 2816
NORM_EPS = 1e-6
HEAD_DIM = 64
N_HEADS = 16
ATTN_SCALE = HEAD_DIM ** -0.5
NEG_INF = -1e30
CONV_A = 31
CONV_D = 3
B_PAIRS = ((128, 1), (512, 4), (2048, 16))
REL_BUCKETS = 32
REL_MAX_DIST = 2048
PAGE = 128
DEC_T = 8

LANES = 128
HALO = 32
NEW_PAD = 16
VMEM_LIMIT = 56 * 1024 * 1024

NT_DIMS = (((1,), (1,)), ((), ()))


def _cparams(*sem):
    return pltpu.CompilerParams(dimension_semantics=sem, vmem_limit_bytes=VMEM_LIMIT)


def _dot(a, b):
    return jnp.dot(a, b, preferred_element_type=F32)


def _dot_nt(a, b):
    return lax.dot_general(a, b, NT_DIMS, preferred_element_type=F32)


def _dot_exact(a, b, dims=None):
    if dims is None:
        return jnp.dot(a, b, preferred_element_type=F32, precision=lax.Precision.HIGHEST)
    return lax.dot_general(a, b, dims, preferred_element_type=F32, precision=lax.Precision.HIGHEST)


def _norm_mod(x, g, shift, scale):
    y = x * lax.rsqrt(jnp.mean(x * x, axis=-1, keepdims=True) + NORM_EPS) * g
    return y * (1.0 + scale) + shift


def _ada_kernel(c_ref, w_ref, b_ref, o_ref):
    c = c_ref[...]
    h = (c * jax.nn.sigmoid(c)).astype(BF16)
    o_ref[...] = _dot(h, w_ref[...].astype(BF16)) + b_ref[...]


def ada_mod(c_all, ada_w, ada_b):
    depth, _, n = ada_w.shape
    nb = c_all.shape[0]
    tn = 1024
    return pl.pallas_call(
        _ada_kernel,
        grid=(depth, n // tn),
        in_specs=[
            pl.BlockSpec((nb, D_MODEL), lambda l, j: (0, 0)),
            pl.BlockSpec((None, D_MODEL, tn), lambda l, j: (l, 0, j)),
            pl.BlockSpec((None, 1, tn), lambda l, j: (l, 0, j)),
        ],
        out_specs=pl.BlockSpec((None, nb, tn), lambda l, j: (l, 0, j)),
        out_shape=jax.ShapeDtypeStruct((depth, nb, n), F32),
        compiler_params=_cparams("parallel", "parallel"),
        name="ada_mod",
    )(c_all, ada_w, ada_b.reshape(depth, 1, n))


def _ffn_kernel(has_final, x_ref, g_ref, sh_ref, sc_ref, gt_ref, w1_ref, w3_ref, w2_ref, *rest):
    if has_final:
        fg_ref, o_ref, h_scr, acc_scr = rest
    else:
        o_ref, h_scr, acc_scr = rest
    f = pl.program_id(1)

    @pl.when(f == 0)
    def _():
        h_scr[...] = _norm_mod(x_ref[...], g_ref[...], sh_ref[...], sc_ref[...]).astype(BF16)
        acc_scr[...] = jnp.zeros_like(acc_scr)

    h = h_scr[...]
    g = _dot(h, w1_ref[...])
    up = _dot(h, w3_ref[...])
    a = (g * jax.nn.sigmoid(g) * up).astype(BF16)
    acc_scr[...] += _dot(a, w2_ref[...])

    @pl.when(f == pl.num_programs(1) - 1)
    def _():
        y = x_ref[...] + (0.5 * gt_ref[...]) * acc_scr[...]
        if has_final:
            y = y * lax.rsqrt(jnp.mean(y * y, axis=-1, keepdims=True) + NORM_EPS) * fg_ref[...]
        o_ref[...] = y


def ffn(x, mod4, j, g_row, w13, w2, tm, final_g=None):
    n = x.shape[0]
    fc = 256
    nf = D_FF // fc
    tpb = (n // mod4.shape[0]) // tm
    r = mod4.shape[2]
    mspec = lambda idx: pl.BlockSpec((None, None, r, D_MODEL), lambda i, f: (i // tpb, idx, 0, 0))
    in_specs = [
        pl.BlockSpec((tm, D_MODEL), lambda i, f: (i, 0)),
        pl.BlockSpec((1, D_MODEL), lambda i, f: (0, 0)),
        mspec(3 * j), mspec(3 * j + 1), mspec(3 * j + 2),
        pl.BlockSpec((D_MODEL, fc), lambda i, f: (0, f)),
        pl.BlockSpec((D_MODEL, fc), lambda i, f: (0, f + nf)),
        pl.BlockSpec((fc, D_MODEL), lambda i, f: (f, 0)),
    ]
    args = [x, g_row.reshape(1, D_MODEL), mod4, mod4, mod4, w13, w13, w2]
    if final_g is not None:
        in_specs.append(pl.BlockSpec((1, D_MODEL), lambda i, f: (0, 0)))
        args.append(final_g.reshape(1, D_MODEL))
    return pl.pallas_call(
        functools.partial(_ffn_kernel, final_g is not None),
        grid=(n // tm, nf),
        in_specs=in_specs,
        out_specs=pl.BlockSpec((tm, D_MODEL), lambda i, f: (i, 0)),
        out_shape=jax.ShapeDtypeStruct((n, D_MODEL), F32),
        scratch_shapes=[pltpu.VMEM((tm, D_MODEL), BF16), pltpu.VMEM((tm, D_MODEL), F32)],
        compiler_params=_cparams("parallel", "arbitrary"),
        name="ffn",
    )(*args)


def _nmm_kernel(mode, nw, has_bias, x_ref, g_ref, sh_ref, sc_ref, *rest):
    w_refs = rest[:nw]
    rest = rest[nw:]
    b_refs = rest[:nw] if has_bias else ()
    rest = rest[len(b_refs):]
    nout = 2 if mode == "gate3" else 1
    o_refs = rest[:nout]
    h_scr = rest[nout]

    @pl.when(pl.program_id(1) == 0)
    def _():
        h_scr[...] = _norm_mod(x_ref[...], g_ref[...], sh_ref[...], sc_ref[...]).astype(BF16)

    h = h_scr[...]
    ps = [_dot(h, w[...]) for w in w_refs]
    if has_bias:
        ps = [p + b[...] for p, b in zip(ps, b_refs)]
    if mode == "plain":
        o_refs[0][...] = ps[0]
    elif mode == "glu":
        o_refs[0][...] = ps[0] * jax.nn.sigmoid(ps[1])
    else:
        o_refs[0][...] = ps[0]
        o_refs[1][...] = ps[1] * ps[2]


def nmm(x, mod4, j, g_row, w, col_offsets, ncol, tn, tm, bias=None, mode="plain"):
    n = x.shape[0]
    nw = len(col_offsets)
    tpb = (n // mod4.shape[0]) // tm
    r = mod4.shape[2]
    mspec = lambda idx: pl.BlockSpec((None, None, r, D_MODEL), lambda i, c: (i // tpb, idx, 0, 0))
    in_specs = [
        pl.BlockSpec((tm, D_MODEL), lambda i, c: (i, 0)),
        pl.BlockSpec((1, D_MODEL), lambda i, c: (0, 0)),
        mspec(3 * j), mspec(3 * j + 1),
    ]
    args = [x, g_row.reshape(1, D_MODEL), mod4, mod4]
    for off in col_offsets:
        in_specs.append(pl.BlockSpec((D_MODEL, tn), lambda i, c, off=off: (0, off + c)))
        args.append(w)
    if bias is not None:
        b2 = bias.reshape(1, -1)
        for off in col_offsets:
            in_specs.append(pl.BlockSpec((1, tn), lambda i, c, off=off: (0, off + c)))
            args.append(b2)
    nout = 2 if mode == "gate3" else 1
    out_specs = [pl.BlockSpec((tm, tn), lambda i, c: (i, c)) for _ in range(nout)]
    out_shape = [jax.ShapeDtypeStruct((n, ncol * tn), F32) for _ in range(nout)]
    outs = pl.pallas_call(
        functools.partial(_nmm_kernel, mode, nw, bias is not None),
        grid=(n // tm, ncol),
        in_specs=in_specs,
        out_specs=out_specs,
        out_shape=out_shape,
        scratch_shapes=[pltpu.VMEM((tm, D_MODEL), BF16)],
        compiler_params=_cparams("parallel", "arbitrary"),
        name="nmm_" + mode,
    )(*args)
    return outs if nout == 2 else outs[0]


def _mmr_kernel(merge, has_bias, *refs):
    if merge:
        o1, o2, o3, l1, l2, l3 = refs[:6]
        refs = refs[6:]
        a1, a2, a3 = l1[...], l2[...], l3[...]
        m = jnp.maximum(jnp.maximum(a1, a2), a3)
        e1, e2, e3 = jnp.exp(a1 - m), jnp.exp(a2 - m), jnp.exp(a3 - m)
        inv = 1.0 / (e1 + e2 + e3)
        a = ((e1 * inv) * o1[...] + (e2 * inv) * o2[...] + (e3 * inv) * o3[...]).astype(BF16)
    else:
        a = refs[0][...].astype(BF16)
        refs = refs[1:]
    if has_bias:
        x_ref, gt_ref, w_ref, b_ref, o_ref = refs
    else:
        x_ref, gt_ref, w_ref, o_ref = refs
    y = _dot(a, w_ref[...])
    if has_bias:
        y = y + b_ref[...]
    o_ref[...] = x_ref[...] + gt_ref[...] * y


def mmr(a_list, x, mod4, j, w, tm, bias=None):
    n = x.shape[0]
    merge = len(a_list) == 6
    tpb = (n // mod4.shape[0]) // tm
    r = mod4.shape[2]
    tile = pl.BlockSpec((tm, D_MODEL), lambda i: (i, 0))
    in_specs = [tile for _ in a_list] + [
        tile,
        pl.BlockSpec((None, None, r, D_MODEL), lambda i: (i // tpb, 3 * j + 2, 0, 0)),
        pl.BlockSpec((D_MODEL, D_MODEL), lambda i: (0, 0)),
    ]
    args = list(a_list) + [x, mod4, w]
    if bias is not None:
        in_specs.append(pl.BlockSpec((1, D_MODEL), lambda i: (0, 0)))
        args.append(bias.reshape(1, D_MODEL))
    return pl.pallas_call(
        functools.partial(_mmr_kernel, merge, bias is not None),
        grid=(n // tm,),
        in_specs=in_specs,
        out_specs=tile,
        out_shape=jax.ShapeDtypeStruct((n, D_MODEL), F32),
        compiler_params=_cparams("parallel"),
        name="mmr_merge" if merge else "mmr",
    )(*args)


def _conv_kernel(mode, ktaps, tm, rc, zero_first, halo_ref, u_ref, w_ref, *rest):
    if mode == "A":
        bdw_ref, lng_ref, lnb_ref, o_ref, cat = rest
    else:
        bg_ref, o_ref, cat = rest
    if zero_first:
        first = pl.program_id(1) == 0

        @pl.when(first)
        def _():
            cat[0:HALO, :] = jnp.zeros((HALO, D_MODEL), F32)

        @pl.when(jnp.logical_not(first))
        def _():
            cat[0:HALO, :] = halo_ref[...]
    else:
        cat[0:HALO, :] = halo_ref[...]
    cat[HALO:HALO + tm, :] = u_ref[...]
    base = HALO - (ktaps - 1)
    for r0 in range(0, tm, rc):
        acc = None
        for k in range(ktaps):
            term = cat[base + r0 + k:base + r0 + k + rc, :] * w_ref[k:k + 1, :]
            acc = term if acc is None else acc + term
        if mode == "A":
            y = acc + bdw_ref[...]
            mu = jnp.mean(y, axis=-1, keepdims=True)
            yc = y - mu
            var = jnp.mean(yc * yc, axis=-1, keepdims=True)
            yn = yc * lax.rsqrt(var + NORM_EPS) * lng_ref[...] + lnb_ref[...]
            o_ref[r0:r0 + rc, :] = (yn * jax.nn.sigmoid(yn)).astype(BF16)
        else:
            o_ref[r0:r0 + rc, :] = (bg_ref[r0:r0 + rc, :] * acc).astype(BF16)


def conv_mix(mode, u3, halo3, taps, tm, zero_first, extras):
    nb, t, _ = u3.shape
    ktaps = taps.shape[0]
    kpad = -(-ktaps // 8) * 8
    taps = jnp.pad(taps, ((0, kpad - ktaps), (0, 0)))
    rc = min(32, tm)
    hb = tm // HALO
    tile = pl.BlockSpec((None, tm, D_MODEL), lambda b, i: (b, i, 0))
    row = pl.BlockSpec((1, D_MODEL), lambda b, i: (0, 0))
    if zero_first:
        halo_spec = pl.BlockSpec((None, HALO, D_MODEL), lambda b, i: (b, jnp.maximum(i * hb - 1, 0), 0))
    else:
        halo_spec = pl.BlockSpec((None, HALO, D_MODEL), lambda b, i: (b, 0, 0))
    in_specs = [halo_spec, tile, pl.BlockSpec((kpad, D_MODEL), lambda b, i: (0, 0))]
    args = [halo3, u3, taps]
    if mode == "A":
        in_specs += [row, row, row]
        args += [e.reshape(1, D_MODEL) for e in extras]
    else:
        in_specs += [tile]
        args += list(extras)
    return pl.pallas_call(
        functools.partial(_conv_kernel, mode, ktaps, tm, rc, zero_first),
        grid=(nb, t // tm),
        in_specs=in_specs,
        out_specs=tile,
        out_shape=jax.ShapeDtypeStruct((nb, t, D_MODEL), BF16),
        scratch_shapes=[pltpu.VMEM((HALO + tm, D_MODEL), F32)],
        compiler_params=_cparams("parallel", "arbitrary"),
        name="conv_" + mode,
    )(*args)


def _dil_attn_kernel(q_ref, kp_ref, kc_ref, vp_ref, vc_ref, bias_ref, o_ref, lse_ref):
    bq = q_ref.shape[0]
    i = pl.program_id(2)
    lane = lax.broadcasted_iota(jnp.int32, (1, LANES), 1)
    lo = lane < HEAD_DIM
    col = lax.broadcasted_iota(jnp.int32, (1, 2 * bq), 1)
    dead = col < jnp.where(i == 0, bq, 0)
    for hp in range(N_HEADS // 2):
        sl = slice(LANES * hp, LANES * (hp + 1))
        q = q_ref[:, sl] * ATTN_SCALE
        k = jnp.concatenate([kp_ref[:, sl], kc_ref[:, sl]], axis=0).astype(BF16)
        v = jnp.concatenate([vp_ref[:, sl], vc_ref[:, sl]], axis=0).astype(BF16)
        outs, lses = [], []
        for e in range(2):
            qe = jnp.where(lo if e == 0 else jnp.logical_not(lo), q, 0.0).astype(BF16)
            s = _dot_nt(qe, k) + bias_ref[2 * hp + e]
            s = jnp.where(dead, NEG_INF, s)
            m = jnp.max(s, axis=-1, keepdims=True)
            p = jnp.exp(s - m)
            l = jnp.sum(p, axis=-1, keepdims=True)
            outs.append(_dot(p.astype(BF16), v) / l)
            lses.append(m + jnp.log(l))
        o_ref[:, sl] = jnp.where(lo, outs[0], outs[1])
        lse_ref[:, sl] = jnp.where(lo, lses[0], lses[1])


def dilated_prompt_attn(qkv3, bias, g, dil):
    bn, s_len, ncols = qkv3.shape
    L = s_len // dil
    bq = 128
    nblk_cols = ncols // D_MODEL
    ng = len(B_PAIRS)
    view = qkv3.reshape(bn, L, dil * ncols)

    def spec(which, prev):
        def imap(b, r, i):
            blk = jnp.maximum(i - 1, 0) if prev else i
            return (b, blk, r * nblk_cols + which * ng + g)
        return pl.BlockSpec((None, bq, D_MODEL), imap)

    out_spec = pl.BlockSpec((None, bq, D_MODEL), lambda b, r, i: (b, i, r))
    o, lse = pl.pallas_call(
        _dil_attn_kernel,
        grid=(bn, dil, L // bq),
        in_specs=[spec(0, False), spec(1, True), spec(1, False), spec(2, True), spec(2, False),
                  pl.BlockSpec((N_HEADS, bq, 2 * bq), lambda b, r, i: (0, 0, 0))],
        out_specs=[out_spec, out_spec],
        out_shape=[jax.ShapeDtypeStruct((bn, L, dil * D_MODEL), F32)] * 2,
        compiler_params=_cparams("parallel", "parallel", "arbitrary"),
        name="dilated_prompt",
    )(view, view, view, view, view, bias)
    return o.reshape(bn * s_len, D_MODEL), lse.reshape(bn * s_len, D_MODEL)


def _head_mask(ncols):
    row = lax.broadcasted_iota(jnp.int32, (N_HEADS * DEC_T, ncols), 0)
    lane = lax.broadcasted_iota(jnp.int32, (N_HEADS * DEC_T, ncols), 1)
    return (row >> 3) == (lane >> 6)


def _build_qbd(q8):
    q = q8 * ATTN_SCALE
    qt = jnp.concatenate([q] * N_HEADS, axis=0)
    return jnp.where(_head_mask(D_MODEL), qt, 0.0).astype(BF16)


def _fold_heads(x):
    xm = jnp.where(_head_mask(D_MODEL), x, 0.0)
    out = xm[0:DEC_T, :]
    for h in range(1, N_HEADS):
        out = out + xm[h * DEC_T:(h + 1) * DEC_T, :]
    return out


def _decode_init(q_ref, new_ref, bnew_ref, qbd_scr, m_scr, l_scr, acc_scr):
    qbd = _build_qbd(q_ref[...])
    qbd_scr[...] = qbd
    kn = new_ref[:, 0:D_MODEL].astype(BF16)
    vn = new_ref[:, D_MODEL:2 * D_MODEL].astype(BF16)
    s = _dot_nt(qbd, kn) + bnew_ref[...]
    m = jnp.max(s, axis=-1, keepdims=True)
    p = jnp.exp(s - m)
    m_scr[...] = m
    l_scr[...] = jnp.sum(p, axis=-1, keepdims=True)
    acc_scr[...] = _dot(p.astype(BF16), vn)


def _decode_update(kv, bias, qbd_scr, m_scr, l_scr, acc_scr):
    kc = kv[:, 0:D_MODEL].astype(BF16)
    vc = kv[:, D_MODEL:2 * D_MODEL].astype(BF16)
    s = _dot_nt(qbd_scr[...], kc) + bias
    m_old = m_scr[...]
    m_new = jnp.maximum(m_old, jnp.max(s, axis=-1, keepdims=True))
    alpha = jnp.exp(m_old - m_new)
    p = jnp.exp(s - m_new)
    l_scr[...] = alpha * l_scr[...] + jnp.sum(p, axis=-1, keepdims=True)
    acc_scr[...] = alpha * acc_scr[...] + _dot(p.astype(BF16), vc)
    m_scr[...] = m_new


def _bdec_kernel(q_ref, new_ref, bnew_ref, buf_ref, bias_ref, o_ref, lse_ref,
                 qbd_scr, m_scr, l_scr, acc_scr):
    c = pl.program_id(1)

    @pl.when(c == 0)
    def _():
        _decode_init(q_ref, new_ref, bnew_ref, qbd_scr, m_scr, l_scr, acc_scr)

    _decode_update(buf_ref[...], bias_ref[...], qbd_scr, m_scr, l_scr, acc_scr)

    @pl.when(c == pl.num_programs(1) - 1)
    def _():
        l = l_scr[...]
        o_ref[...] = _fold_heads(acc_scr[...] / l)
        lse_ref[...] = _fold_heads(jnp.broadcast_to(m_scr[...] + jnp.log(l), (N_HEADS * DEC_T, D_MODEL)))


def dilated_decode(q3, new3, bias_new, buf3, bias_buf):
    bn, L, _ = buf3.shape
    rows = N_HEADS * DEC_T
    rchunk = min(L, 512)
    o, lse = pl.pallas_call(
        _bdec_kernel,
        grid=(bn, L // rchunk),
        in_specs=[
            pl.BlockSpec((None, DEC_T, D_MODEL), lambda b, c: (b, 0, 0)),
            pl.BlockSpec((None, NEW_PAD, 2 * D_MODEL), lambda b, c: (b, 0, 0)),
            pl.BlockSpec((rows, NEW_PAD), lambda b, c: (0, 0)),
            pl.BlockSpec((None, rchunk, 2 * D_MODEL), lambda b, c: (b, c, 0)),
            pl.BlockSpec((rows, rchunk), lambda b, c: (0, c)),
        ],
        out_specs=[pl.BlockSpec((None, DEC_T, D_MODEL), lambda b, c: (b, 0, 0))] * 2,
        out_shape=[jax.ShapeDtypeStruct((bn, DEC_T, D_MODEL), F32)] * 2,
        scratch_shapes=[pltpu.VMEM((rows, D_MODEL), BF16), pltpu.VMEM((rows, 1), F32),
                        pltpu.VMEM((rows, 1), F32), pltpu.VMEM((rows, D_MODEL), F32)],
        compiler_params=_cparams("parallel", "arbitrary"),
        name="dilated_decode",
    )(q3, new3, bias_new, buf3, bias_buf)
    return o.reshape(bn * DEC_T, D_MODEL), lse.reshape(bn * DEC_T, D_MODEL)


def _fgate_kernel(seg_shift, use_carry, p_ref, b_ref, logf_ref, cum_ref, carry):
    tt = p_ref.shape[0]
    x = p_ref[...] + b_ref[...]
    lf = jnp.minimum(x, 0.0) - jnp.log(1.0 + jnp.exp(-jnp.abs(x)))
    logf_ref[...] = lf
    r = lax.broadcasted_iota(jnp.int32, (tt, tt), 0)
    c = lax.broadcasted_iota(jnp.int32, (tt, tt), 1)
    tri = jnp.where(c <= r, 1.0, 0.0)
    if seg_shift is not None:
        tri = jnp.where((r >> seg_shift) == (c >> seg_shift), tri, 0.0)
    cum = _dot_exact(tri, lf)
    if use_carry:
        @pl.when(pl.program_id(1) == 0)
        def _():
            carry[...] = jnp.zeros_like(carry)

        cum = cum + carry[...]
        carry[...] = cum[tt - 1:tt, :]
    cum_ref[...] = cum


def fgate(p3, b_f, tt, seg_shift, use_carry):
    bn, t, ncols = p3.shape
    fcol = (ncols - LANES) // LANES
    b_pad = jnp.pad(b_f.reshape(1, -1), ((0, 0), (0, LANES - b_f.shape[-1])))
    out_spec = pl.BlockSpec((None, tt, LANES), lambda b, i: (b, i, 0))
    return pl.pallas_call(
        functools.partial(_fgate_kernel, seg_shift, use_carry),
        grid=(bn, t // tt),
        in_specs=[pl.BlockSpec((None, tt, LANES), lambda b, i: (b, i, fcol)),
                  pl.BlockSpec((1, LANES), lambda b, i: (0, 0))],
        out_specs=[out_spec, out_spec],
        out_shape=[jax.ShapeDtypeStruct((bn, t, LANES), F32)] * 2,
        scratch_shapes=[pltpu.VMEM((1, LANES), F32)],
        compiler_params=_cparams("parallel", "arbitrary"),
        name="fgate",
    )(p3, b_pad)


def _fox_flash_kernel(tq, q_ref, k_ref, v_ref, cq_ref, ck_ref, o_ref, m_scr, l_scr, acc_scr):
    hp = pl.program_id(1)
    qi = pl.program_id(2)
    lane = lax.broadcasted_iota(jnp.int32, (1, LANES), 1)
    lo = lane < HEAD_DIM
    q = q_ref[...] * ATTN_SCALE
    qe = [jnp.where(lo, q, 0.0).astype(BF16), jnp.where(lo, 0.0, q).astype(BF16)]
    cum = cq_ref[...]
    cq = [jnp.sum(jnp.where(lane == 2 * hp + e, cum, 0.0), axis=-1, keepdims=True) for e in range(2)]
    m_scr[...] = jnp.full(m_scr.shape, NEG_INF, F32)
    l_scr[...] = jnp.zeros_like(l_scr)
    acc_scr[...] = jnp.zeros_like(acc_scr)

    def tile(kt, diagonal):
        ks = pl.multiple_of(kt * tq, tq)
        k = k_ref[pl.ds(ks, tq), :].astype(BF16)
        v = v_ref[pl.ds(ks, tq), :].astype(BF16)
        ck = ck_ref[kt]
        for e in range(2):
            s = _dot_nt(qe[e], k) + cq[e] - ck[e:e + 1, :]
            if diagonal:
                r = lax.broadcasted_iota(jnp.int32, (tq, tq), 0)
                c = lax.broadcasted_iota(jnp.int32, (tq, tq), 1)
                s = jnp.where(r >= c, s, NEG_INF)
            m_old = m_scr[e]
            m_new = jnp.maximum(m_old, jnp.max(s, axis=-1, keepdims=True))
            alpha = jnp.exp(m_old - m_new)
            p = jnp.exp(s - m_new)
            l_scr[e] = alpha * l_scr[e] + jnp.sum(p, axis=-1, keepdims=True)
            acc_scr[e] = alpha * acc_scr[e] + _dot(p.astype(BF16), v)
            m_scr[e] = m_new

    def body(kt, carry):
        tile(kt, False)
        return carry

    lax.fori_loop(0, qi, body, 0)
    tile(qi, True)
    o_ref[...] = jnp.where(lo, acc_scr[0] / l_scr[0], acc_scr[1] / l_scr[1]).astype(BF16)


def fox_flash(p3, cum3, ck5):
    bn, s_len, _ = p3.shape
    tq = 512
    nh2 = N_HEADS // 2
    return pl.pallas_call(
        functools.partial(_fox_flash_kernel, tq),
        grid=(bn, nh2, s_len // tq),
        in_specs=[
            pl.BlockSpec((None, tq, LANES), lambda b, h, i: (b, i, h)),
            pl.BlockSpec((None, s_len, LANES), lambda b, h, i: (b, 0, nh2 + h)),
            pl.BlockSpec((None, s_len, LANES), lambda b, h, i: (b, 0, 2 * nh2 + h)),
            pl.BlockSpec((None, tq, LANES), lambda b, h, i: (b, i, 0)),
            pl.BlockSpec((None, None, s_len // tq, 2, tq), lambda b, h, i: (b, h, 0, 0, 0)),
        ],
        out_specs=pl.BlockSpec((None, tq, LANES), lambda b, h, i: (b, i, h)),
        out_shape=jax.ShapeDtypeStruct((bn, s_len, D_MODEL), BF16),
        scratch_shapes=[pltpu.VMEM((2, tq, 1), F32), pltpu.VMEM((2, tq, 1), F32),
                        pltpu.VMEM((2, tq, LANES), F32)],
        compiler_params=_cparams("parallel", "parallel", "arbitrary"),
        name="fox_flash",
    )(p3, p3, p3, cum3, ck5)


def _foxdec_kernel(npg, pt_ref, q_ref, new_ref, cncol_ref, bnew_ref, *rest):
    kv_refs = rest[:npg]
    lf_refs = rest[npg:2 * npg]
    o_ref, qbd_scr, m_scr, l_scr, acc_scr, carry, lfpad = rest[2 * npg:]
    j = pl.program_id(1)
    rows = N_HEADS * DEC_T

    @pl.when(j == 0)
    def _():
        _decode_init(q_ref, new_ref, bnew_ref, qbd_scr, m_scr, l_scr, acc_scr)
        carry[...] = jnp.zeros_like(carry)
        lfpad[...] = jnp.zeros_like(lfpad)

    r = lax.broadcasted_iota(jnp.int32, (rows, PAGE), 0)
    c = lax.broadcasted_iota(jnp.int32, (rows, PAGE), 1)
    expand = jnp.where((r >> 3) == c, 1.0, 0.0)
    later = jnp.where(r > c, 1.0, 0.0)
    for i in range(npg):
        lfpad[:, 0:N_HEADS] = lf_refs[i][...]
        lt = _dot_exact(expand, lfpad[...], NT_DIMS)
        bias = cncol_ref[...] + carry[...] + _dot_exact(lt, later)
        carry[...] = carry[...] + jnp.sum(lt, axis=-1, keepdims=True)
        _decode_update(kv_refs[i][...], bias, qbd_scr, m_scr, l_scr, acc_scr)

    @pl.when(j == pl.num_programs(1) - 1)
    def _():
        o_ref[...] = _fold_heads(acc_scr[...] / l_scr[...])


def fox_decode(page_table, q3, new3, cn_col, bias_new, cache_kv4, cache_lf4, u):
    bn, n_pages = page_table.shape
    npg = 4
    rows = N_HEADS * DEC_T

    def page_spec(i, width):
        def imap(b, j, pt):
            return (u, pt[b, n_pages - 1 - (j * npg + i)], 0, 0)
        return pl.BlockSpec((None, None, PAGE, width), imap)

    per_b = lambda shape: pl.BlockSpec((None,) + shape, lambda b, j, pt: (b, 0, 0))
    grid_spec = pltpu.PrefetchScalarGridSpec(
        num_scalar_prefetch=1,
        grid=(bn, n_pages // npg),
        in_specs=[per_b((DEC_T, D_MODEL)), per_b((NEW_PAD, 2 * D_MODEL)), per_b((rows, 1)),
                  per_b((rows, NEW_PAD))]
                 + [page_spec(i, 2 * D_MODEL) for i in range(npg)]
                 + [page_spec(i, N_HEADS) for i in range(npg)],
        out_specs=per_b((DEC_T, D_MODEL)),
        scratch_shapes=[pltpu.VMEM((rows, D_MODEL), BF16), pltpu.VMEM((rows, 1), F32),
                        pltpu.VMEM((rows, 1), F32), pltpu.VMEM((rows, D_MODEL), F32),
                        pltpu.VMEM((rows, 1), F32), pltpu.VMEM((PAGE, LANES), F32)],
    )
    o = pl.pallas_call(
        functools.partial(_foxdec_kernel, npg),
        grid_spec=grid_spec,
        out_shape=jax.ShapeDtypeStruct((bn, DEC_T, D_MODEL), F32),
        compiler_params=_cparams("parallel", "arbitrary"),
        name="fox_decode",
    )(page_table, q3, new3, cn_col, bias_new, *([cache_kv4] * npg), *([cache_lf4] * npg))
    return o.reshape(bn * DEC_T, D_MODEL)


def _t5_bucket(dist):
    max_exact = REL_BUCKETS // 2
    df = jnp.maximum(dist, 1).astype(F32)
    large = max_exact + (jnp.log(df / max_exact) / math.log(REL_MAX_DIST / max_exact)
                         * (REL_BUCKETS - max_exact)).astype(jnp.int32)
    large = jnp.minimum(large, REL_BUCKETS - 1)
    return jnp.where(dist < max_exact, dist, large)


def _prompt_bias(tab, dil, ns, bq):
    steps = jnp.arange(bq)[:, None] - jnp.arange(2 * bq)[None, :] + ns
    valid = (steps >= 0) & (steps <= ns)
    bias = tab[_t5_bucket(jnp.clip(steps, 0, ns) * dil)].transpose(2, 0, 1).astype(F32)
    return jnp.where(valid[None], bias, NEG_INF)


def _decode_bias(tab, dil, ns, lb):
    t = jnp.arange(DEC_T)
    def table(dist):
        valid = (dist >= 0) & (dist % dil == 0) & (dist <= ns * dil)
        b = tab[_t5_bucket(jnp.maximum(dist, 0))].transpose(2, 0, 1).astype(F32)
        return jnp.where(valid[None], b, NEG_INF).reshape(N_HEADS * DEC_T, dist.shape[1])
    d_buf = lb + t[:, None] - jnp.arange(lb)[None, :]
    s = jnp.arange(NEW_PAD)
    d_new = jnp.where(s[None, :] < DEC_T, t[:, None] - s[None, :], -1)
    return table(d_buf), table(d_new)


def kernel(x_prompt, x_sample, c_prompt, c_sample, state_a_conv, state_b_kv_w128, state_b_kv_w512, state_b_kv_w2048, cache_c_kv, cache_c_logf, page_table, state_d_conv, ada_w, ada_b, norm_g, final_g, ffn_w13, ffn_w2, a_w_in, a_b_in, a_w_dw, a_b_dw, a_ln_g, a_ln_b, a_w_out, a_b_out, b_w_qkv, b_w_out, rel_bias, c_w_in, c_b_f, c_w_out, d_w_in, d_w_conv, d_w_out):
    bn, s_len, _ = x_prompt.shape
    dn, t_new, _ = x_sample.shape
    assert t_new == DEC_T and s_len % 512 == 0
    n_p, n_s = bn * s_len, dn * t_new
    depth = ada_w.shape[0]
    tm_p, tm_s = 1024, n_s
    hd = N_HEADS * HEAD_DIM
    b_states = (state_b_kv_w128, state_b_kv_w512, state_b_kv_w2048)

    mod = ada_mod(jnp.concatenate([c_prompt, c_sample], axis=0), ada_w, ada_b)
    mod = mod.reshape(depth, bn + dn, 9, D_MODEL)

    xp = x_prompt.reshape(n_p, D_MODEL)
    xs = x_sample.reshape(n_s, D_MODEL)
    outs = {}
    for l in range(depth):
        kind, u = l % 4, l // 4
        mod_p = mod[l, :bn].reshape(bn, 9, 1, D_MODEL)
        mod_s = jnp.repeat(mod[l, bn:], t_new, axis=0).transpose(1, 0, 2)[None]
        paths = ((xp, mod_p, tm_p), (xs, mod_s, tm_s))

        w13, w2 = ffn_w13[l, 0].astype(BF16), ffn_w2[l, 0].astype(BF16)
        xp, xs = [ffn(x, m, 0, norm_g[l, 0], w13, w2, tm) for x, m, tm in paths]
        paths = ((xp, mod_p, tm_p), (xs, mod_s, tm_s))

        if kind == 0:
            w_in, w_out = a_w_in[u].astype(BF16), a_w_out[u].astype(BF16)
            tn = 512
            us = [nmm(x, m, 1, norm_g[l, 1], w_in, (0, D_MODEL // tn), D_MODEL // tn, tn, tm,
                      bias=a_b_in[u], mode="glu") for x, m, tm in paths]
            up3, us3 = us[0].reshape(bn, s_len, D_MODEL), us[1].reshape(dn, t_new, D_MODEL)
            extras = (a_b_dw[u], a_ln_g[u], a_ln_b[u])
            yp = conv_mix("A", up3, up3, a_w_dw[u], 256, True, extras)
            halo_s = jnp.pad(state_a_conv[u], ((0, 0), (HALO - (CONV_A - 1), 0), (0, 0)))
            ys = conv_mix("A", us3, halo_s, a_w_dw[u], t_new, False, extras)
            xp = mmr([yp.reshape(n_p, D_MODEL)], xp, mod_p, 1, w_out, tm_p, bias=a_b_out[u])
            xs = mmr([ys.reshape(n_s, D_MODEL)], xs, mod_s, 1, w_out, tm_s, bias=a_b_out[u])
            outs.setdefault("a_p", []).append(up3[:, s_len - (CONV_A - 1):])
            outs.setdefault("a_s", []).append(
                jnp.concatenate([state_a_conv[u], us3], axis=1)[:, t_new:])
        elif kind == 1:
            w_qkv, w_out = b_w_qkv[u].astype(BF16), b_w_out[u].astype(BF16)
            ncol = w_qkv.shape[1] // 1024
            qkv_p, qkv_s = [nmm(x, m, 1, norm_g[l, 1], w_qkv, (0,), ncol, 1024, tm)
                            for x, m, tm in paths]
            qkv_p3 = qkv_p.reshape(bn, s_len, -1)
            qkv_s3 = qkv_s.reshape(dn, t_new, -1)
            ng = len(B_PAIRS)
            col = lambda which, g: slice((which * ng + g) * hd, (which * ng + g + 1) * hd)
            po, pl_, so, sl_ = [], [], [], []
            for g, (win, dil) in enumerate(B_PAIRS):
                ns = win // dil
                tab = rel_bias[:, g * N_HEADS:(g + 1) * N_HEADS]
                o, lse = dilated_prompt_attn(qkv_p3, _prompt_bias(tab, dil, ns, 128), g, dil)
                po.append(o)
                pl_.append(lse)
                lw = min(win, s_len)
                outs.setdefault("b%d_p" % g, []).append(
                    jnp.stack([qkv_p3[:, s_len - lw:, col(1, g)], qkv_p3[:, s_len - lw:, col(2, g)]],
                              axis=2).reshape(bn, lw, 2, N_HEADS, HEAD_DIM))
                buf = b_states[g][u]
                lb = buf.shape[1]
                buf3 = buf.reshape(dn, lb, 2 * hd)
                new_kv = jnp.concatenate([qkv_s3[:, :, col(1, g)], qkv_s3[:, :, col(2, g)]], axis=-1)
                bias_buf, bias_new = _decode_bias(tab, dil, ns, lb)
                o, lse = dilated_decode(qkv_s3[:, :, col(0, g)],
                                        jnp.pad(new_kv, ((0, 0), (0, NEW_PAD - t_new), (0, 0))),
                                        bias_new, buf3, bias_buf)
                so.append(o)
                sl_.append(lse)
                outs.setdefault("b%d_s" % g, []).append(
                    jnp.concatenate([buf3, new_kv], axis=1)[:, t_new:]
                    .reshape(dn, lb, 2, N_HEADS, HEAD_DIM))
            xp = mmr(po + pl_, xp, mod_p, 1, w_out, tm_p // 2)
            xs = mmr(so + sl_, xs, mod_s, 1, w_out, tm_s)
        elif kind == 2:
            nf = c_b_f.shape[-1]
            w_in = jnp.pad(c_w_in[u], ((0, 0), (0, LANES - nf))).astype(BF16)
            w_out = c_w_out[u].astype(BF16)
            tn = 640
            ncol = w_in.shape[1] // tn
            pp, ps = [nmm(x, m, 1, norm_g[l, 1], w_in, (0,), ncol, tn, tm) for x, m, tm in paths]
            pp3 = pp.reshape(bn, s_len, -1)
            tq = 512
            logf_p, cum_p = fgate(pp3, c_b_f[u], tq, None, True)
            ck5 = (cum_p[:, :, :N_HEADS].transpose(0, 2, 1)
                   .reshape(bn, N_HEADS // 2, 2, s_len // tq, tq).transpose(0, 1, 3, 2, 4))
            o_p = fox_flash(pp3, cum_p, ck5)
            xp = mmr([o_p.reshape(n_p, D_MODEL)], xp, mod_p, 1, w_out, tm_p)
            outs.setdefault("ckv_p", []).append(
                pp3[:, :, hd:3 * hd].reshape(bn, s_len // PAGE, PAGE, 2, N_HEADS, HEAD_DIM))
            outs.setdefault("clf_p", []).append(
                logf_p[:, :, :nf].reshape(bn, s_len // PAGE, PAGE, nf))
            logf_s, cn = fgate(ps.reshape(1, n_s, -1), c_b_f[u], n_s, 3, False)
            cn_t = cn[0, :, :N_HEADS].reshape(dn, t_new, N_HEADS).transpose(0, 2, 1)
            cn_col = cn_t.reshape(dn, N_HEADS * t_new, 1)
            cn_keys = jnp.broadcast_to(cn_t[:, :, None, :], (dn, N_HEADS, t_new, t_new))
            cn_keys = cn_keys.reshape(dn, N_HEADS * t_new, t_new)
            tq_idx = jnp.tile(jnp.arange(t_new), N_HEADS)[:, None]
            causal = jnp.arange(t_new)[None, :] <= tq_idx
            bias_new = jnp.where(causal[None], cn_col - cn_keys, NEG_INF)
            bias_new = jnp.pad(bias_new, ((0, 0), (0, 0), (0, NEW_PAD - t_new)), constant_values=NEG_INF)
            ps3 = ps.reshape(dn, t_new, -1)
            new_kv = ps3[:, :, hd:3 * hd]
            o_s = fox_decode(page_table, ps3[:, :, :hd],
                             jnp.pad(new_kv, ((0, 0), (0, NEW_PAD - t_new), (0, 0))),
                             cn_col, bias_new,
                             cache_c_kv.reshape(cache_c_kv.shape[:3] + (2 * hd,)), cache_c_logf, u)
            xs = mmr([o_s], xs, mod_s, 1, w_out, tm_s)
            outs.setdefault("ckv_s", []).append(new_kv.reshape(dn, t_new, 2, N_HEADS, HEAD_DIM))
            outs.setdefault("clf_s", []).append(logf_s[0, :, :nf].reshape(dn, t_new, nf))
        else:
            w_in, w_out = d_w_in[u].astype(BF16), d_w_out[u].astype(BF16)
            tn = 512
            nc = D_MODEL // tn
            (bg_p, z_p), (bg_s, z_s) = [
                nmm(x, m, 1, norm_g[l, 1], w_in, (0, nc, 2 * nc), nc, tn, tm, mode="gate3")
                for x, m, tm in paths]
            zp3, zs3 = z_p.reshape(bn, s_len, D_MODEL), z_s.reshape(dn, t_new, D_MODEL)
            yp = conv_mix("D", zp3, zp3, d_w_conv[u], 256, True, (bg_p.reshape(bn, s_len, D_MODEL),))
            halo_s = jnp.pad(state_d_conv[u], ((0, 0), (HALO - (CONV_D - 1), 0), (0, 0)))
            ys = conv_mix("D", zs3, halo_s, d_w_conv[u], t_new, False, (bg_s.reshape(dn, t_new, D_MODEL),))
            xp = mmr([yp.reshape(n_p, D_MODEL)], xp, mod_p, 1, w_out, tm_p)
            xs = mmr([ys.reshape(n_s, D_MODEL)], xs, mod_s, 1, w_out, tm_s)
            outs.setdefault("d_p", []).append(zp3[:, s_len - (CONV_D - 1):])
            outs.setdefault("d_s", []).append(
                jnp.concatenate([state_d_conv[u], zs3], axis=1)[:, t_new:])

        paths = ((xp, mod_p, tm_p), (xs, mod_s, tm_s))
        w13, w2 = ffn_w13[l, 1].astype(BF16), ffn_w2[l, 1].astype(BF16)
        fg = final_g if l == depth - 1 else None
        xp, xs = [ffn(x, m, 2, norm_g[l, 2], w13, w2, tm, final_g=fg) for x, m, tm in paths]

    st = lambda key: jnp.stack(outs[key])
    return (xp.reshape(bn, s_len, D_MODEL), xs.reshape(dn, t_new, D_MODEL),
            st("a_p"), st("a_s"), st("b0_p"), st("b0_s"), st("b1_p"), st("b1_s"), st("b2_p"), st("b2_s"),
            st("ckv_p"), st("clf_p"), st("ckv_s"), st("clf_s"), st("d_p"), st("d_s"))
```

```python
import functools
import math

import jax
import jax.numpy as jnp
from jax import lax
from jax.experimental import pallas as pl
from jax.experimental.pallas import tpu as pltpu

F32 = jnp.float32
BF16 = jnp.bfloat16

D_MODEL = 1024
D_FF = 2816
NORM_EPS = 1e-6
HEAD_DIM = 64
N_HEADS = 16
ATTN_SCALE = HEAD_DIM ** -0.5
LOG2E = math.log2(math.e)
NEG_INF = -1e30
CONV_A = 31
CONV_D = 3
B_PAIRS = ((128, 1), (512, 4), (2048, 16))
REL_BUCKETS = 32
REL_MAX_DIST = 2048
PAGE = 128
DEC_T = 8

LANES = 128
HALO = 32
NEW_PAD = 16
VMEM_LIMIT = 56 * 1024 * 1024

NT_DIMS = (((1,), (1,)), ((), ()))


def _cparams(*sem):
    return pltpu.CompilerParams(dimension_semantics=sem, vmem_limit_bytes=VMEM_LIMIT)


def _dot(a, b):
    return jnp.dot(a, b, preferred_element_type=F32)


def _dot_nt(a, b):
    return lax.dot_general(a, b, NT_DIMS, preferred_element_type=F32)


def _dot_exact(a, b):
    return jnp.dot(a, b, preferred_element_type=F32, precision=lax.Precision.HIGHEST)


def _norm_mod(x, g, shift, scale):
    y = x * lax.rsqrt(jnp.mean(x * x, axis=-1, keepdims=True) + NORM_EPS) * g
    return y * (1.0 + scale) + shift


def _ada_kernel(c_ref, w_ref, b_ref, o_ref):
    c = c_ref[...]
    h = (c * jax.nn.sigmoid(c)).astype(BF16)
    o_ref[...] = _dot(h, w_ref[...].astype(BF16)) + b_ref[...]


def ada_mod(c_all, ada_w, ada_b):
    depth, _, n = ada_w.shape
    nb = c_all.shape[0]
    tn = 1024
    return pl.pallas_call(
        _ada_kernel,
        grid=(depth, n // tn),
        in_specs=[
            pl.BlockSpec((nb, D_MODEL), lambda l, j: (0, 0)),
            pl.BlockSpec((None, D_MODEL, tn), lambda l, j: (l, 0, j)),
            pl.BlockSpec((None, 1, tn), lambda l, j: (l, 0, j)),
        ],
        out_specs=pl.BlockSpec((None, nb, tn), lambda l, j: (l, 0, j)),
        out_shape=jax.ShapeDtypeStruct((depth, nb, n), F32),
        compiler_params=_cparams("parallel", "parallel"),
        name="ada_mod",
    )(c_all, ada_w, ada_b.reshape(depth, 1, n))


def _ffn_kernel(has_final, x_ref, g_ref, sh_ref, sc_ref, gt_ref, w1_ref, w3_ref, w2_ref, *rest):
    if has_final:
        fg_ref, o_ref, h_scr, acc_scr = rest
    else:
        o_ref, h_scr, acc_scr = rest
    f = pl.program_id(1)

    @pl.when(f == 0)
    def _():
        h_scr[...] = _norm_mod(x_ref[...], g_ref[...], sh_ref[...], sc_ref[...]).astype(BF16)
        acc_scr[...] = jnp.zeros_like(acc_scr)

    h = h_scr[...]
    g = _dot(h, w1_ref[...])
    up = _dot(h, w3_ref[...])
    a = (g * jax.nn.sigmoid(g) * up).astype(BF16)
    acc_scr[...] += _dot(a, w2_ref[...])

    @pl.when(f == pl.num_programs(1) - 1)
    def _():
        y = x_ref[...] + (0.5 * gt_ref[...]) * acc_scr[...]
        if has_final:
            y = y * lax.rsqrt(jnp.mean(y * y, axis=-1, keepdims=True) + NORM_EPS) * fg_ref[...]
        o_ref[...] = y


def ffn(x, mod4, j, g_row, w13, w2, tm, final_g=None):
    n = x.shape[0]
    fc = 256
    nf = D_FF // fc
    tpb = (n // mod4.shape[0]) // tm
    r = mod4.shape[2]
    mspec = lambda idx: pl.BlockSpec((None, None, r, D_MODEL), lambda i, f: (i // tpb, idx, 0, 0))
    in_specs = [
        pl.BlockSpec((tm, D_MODEL), lambda i, f: (i, 0)),
        pl.BlockSpec((1, D_MODEL), lambda i, f: (0, 0)),
        mspec(3 * j), mspec(3 * j + 1), mspec(3 * j + 2),
        pl.BlockSpec((D_MODEL, fc), lambda i, f: (0, f)),
        pl.BlockSpec((D_MODEL, fc), lambda i, f: (0, f + nf)),
        pl.BlockSpec((fc, D_MODEL), lambda i, f: (f, 0)),
    ]
    args = [x, g_row.reshape(1, D_MODEL), mod4, mod4, mod4, w13, w13, w2]
    if final_g is not None:
        in_specs.append(pl.BlockSpec((1, D_MODEL), lambda i, f: (0, 0)))
        args.append(final_g.reshape(1, D_MODEL))
    return pl.pallas_call(
        functools.partial(_ffn_kernel, final_g is not None),
        grid=(n // tm, nf),
        in_specs=in_specs,
        out_specs=pl.BlockSpec((tm, D_MODEL), lambda i, f: (i, 0)),
        out_shape=jax.ShapeDtypeStruct((n, D_MODEL), F32),
        scratch_shapes=[pltpu.VMEM((tm, D_MODEL), BF16), pltpu.VMEM((tm, D_MODEL), F32)],
        compiler_params=_cparams("parallel", "arbitrary"),
        name="ffn",
    )(*args)


def _nmm_kernel(mode, nw, has_bias, x_ref, g_ref, sh_ref, sc_ref, *rest):
    w_refs = rest[:nw]
    rest = rest[nw:]
    b_refs = rest[:nw] if has_bias else ()
    rest = rest[len(b_refs):]
    nout = 2 if mode == "gate3" else 1
    o_refs = rest[:nout]
    h_scr = rest[nout]

    @pl.when(pl.program_id(1) == 0)
    def _():
        h_scr[...] = _norm_mod(x_ref[...], g_ref[...], sh_ref[...], sc_ref[...]).astype(BF16)

    h = h_scr[...]
    ps = [_dot(h, w[...]) for w in w_refs]
    if has_bias:
        ps = [p + b[...] for p, b in zip(ps, b_refs)]
    if mode == "plain":
        o_refs[0][...] = ps[0]
    elif mode == "glu":
        o_refs[0][...] = ps[0] * jax.nn.sigmoid(ps[1])
    else:
        o_refs[0][...] = ps[0]
        o_refs[1][...] = ps[1] * ps[2]


def nmm(x, mod4, j, g_row, w, col_offsets, ncol, tn, tm, bias=None, mode="plain"):
    n = x.shape[0]
    nw = len(col_offsets)
    tpb = (n // mod4.shape[0]) // tm
    r = mod4.shape[2]
    mspec = lambda idx: pl.BlockSpec((None, None, r, D_MODEL), lambda i, c: (i // tpb, idx, 0, 0))
    in_specs = [
        pl.BlockSpec((tm, D_MODEL), lambda i, c: (i, 0)),
        pl.BlockSpec((1, D_MODEL), lambda i, c: (0, 0)),
        mspec(3 * j), mspec(3 * j + 1),
    ]
    args = [x, g_row.reshape(1, D_MODEL), mod4, mod4]
    for off in col_offsets:
        in_specs.append(pl.BlockSpec((D_MODEL, tn), lambda i, c, off=off: (0, off + c)))
        args.append(w)
    if bias is not None:
        b2 = bias.reshape(1, -1)
        for off in col_offsets:
            in_specs.append(pl.BlockSpec((1, tn), lambda i, c, off=off: (0, off + c)))
            args.append(b2)
    nout = 2 if mode == "gate3" else 1
    out_specs = [pl.BlockSpec((tm, tn), lambda i, c: (i, c)) for _ in range(nout)]
    out_shape = [jax.ShapeDtypeStruct((n, ncol * tn), F32) for _ in range(nout)]
    outs = pl.pallas_call(
        functools.partial(_nmm_kernel, mode, nw, bias is not None),
        grid=(n // tm, ncol),
        in_specs=in_specs,
        out_specs=out_specs,
        out_shape=out_shape,
        scratch_shapes=[pltpu.VMEM((tm, D_MODEL), BF16)],
        compiler_params=_cparams("parallel", "arbitrary"),
        name="nmm_" + mode,
    )(*args)
    return outs if nout == 2 else outs[0]


def _mmr_kernel(merge, has_bias, *refs):
    if merge:
        o1, o2, o3, l1, l2, l3 = refs[:6]
        refs = refs[6:]
        a1, a2, a3 = l1[...], l2[...], l3[...]
        m = jnp.maximum(jnp.maximum(a1, a2), a3)
        e1, e2, e3 = jnp.exp(a1 - m), jnp.exp(a2 - m), jnp.exp(a3 - m)
        inv = 1.0 / (e1 + e2 + e3)
        a = ((e1 * inv) * o1[...] + (e2 * inv) * o2[...] + (e3 * inv) * o3[...]).astype(BF16)
    else:
        a = refs[0][...].astype(BF16)
        refs = refs[1:]
    if has_bias:
        x_ref, gt_ref, w_ref, b_ref, o_ref = refs
    else:
        x_ref, gt_ref, w_ref, o_ref = refs
    y = _dot(a, w_ref[...])
    if has_bias:
        y = y + b_ref[...]
    o_ref[...] = x_ref[...] + gt_ref[...] * y


def mmr(a_list, x, mod4, j, w, tm, bias=None):
    n = x.shape[0]
    merge = len(a_list) == 6
    tpb = (n // mod4.shape[0]) // tm
    r = mod4.shape[2]
    tile = pl.BlockSpec((tm, D_MODEL), lambda i: (i, 0))
    in_specs = [tile for _ in a_list] + [
        tile,
        pl.BlockSpec((None, None, r, D_MODEL), lambda i: (i // tpb, 3 * j + 2, 0, 0)),
        pl.BlockSpec((D_MODEL, D_MODEL), lambda i: (0, 0)),
    ]
    args = list(a_list) + [x, mod4, w]
    if bias is not None:
        in_specs.append(pl.BlockSpec((1, D_MODEL), lambda i: (0, 0)))
        args.append(bias.reshape(1, D_MODEL))
    return pl.pallas_call(
        functools.partial(_mmr_kernel, merge, bias is not None),
        grid=(n // tm,),
        in_specs=in_specs,
        out_specs=tile,
        out_shape=jax.ShapeDtypeStruct((n, D_MODEL), F32),
        compiler_params=_cparams("parallel"),
        name="mmr_merge" if merge else "mmr",
    )(*args)


def _conv_kernel(mode, ktaps, tm, rc, zero_first, halo_ref, u_ref, w_ref, *rest):
    if mode == "A":
        bdw_ref, lng_ref, lnb_ref, o_ref, cat = rest
    else:
        bg_ref, o_ref, cat = rest
    if zero_first:
        first = pl.program_id(1) == 0

        @pl.when(first)
        def _():
            cat[0:HALO, :] = jnp.zeros((HALO, D_MODEL), F32)

        @pl.when(jnp.logical_not(first))
        def _():
            cat[0:HALO, :] = halo_ref[...]
    else:
        cat[0:HALO, :] = halo_ref[...]
    cat[HALO:HALO + tm, :] = u_ref[...]
    base = HALO - (ktaps - 1)
    for r0 in range(0, tm, rc):
        acc = None
        for k in range(ktaps):
            term = cat[base + r0 + k:base + r0 + k + rc, :] * w_ref[k:k + 1, :]
            acc = term if acc is None else acc + term
        if mode == "A":
            y = acc + bdw_ref[...]
            mu = jnp.mean(y, axis=-1, keepdims=True)
            yc = y - mu
            var = jnp.mean(yc * yc, axis=-1, keepdims=True)
            yn = yc * lax.rsqrt(var + NORM_EPS) * lng_ref[...] + lnb_ref[...]
            o_ref[r0:r0 + rc, :] = (yn * jax.nn.sigmoid(yn)).astype(BF16)
        else:
            o_ref[r0:r0 + rc, :] = (bg_ref[r0:r0 + rc, :] * acc).astype(BF16)


def conv_mix(mode, u3, halo3, taps, tm, zero_first, extras):
    nb, t, _ = u3.shape
    ktaps = taps.shape[0]
    kpad = -(-ktaps // 8) * 8
    taps = jnp.pad(taps, ((0, kpad - ktaps), (0, 0)))
    rc = min(32, tm)
    hb = tm // HALO
    tile = pl.BlockSpec((None, tm, D_MODEL), lambda b, i: (b, i, 0))
    row = pl.BlockSpec((1, D_MODEL), lambda b, i: (0, 0))
    if zero_first:
        halo_spec = pl.BlockSpec((None, HALO, D_MODEL), lambda b, i: (b, jnp.maximum(i * hb - 1, 0), 0))
    else:
        halo_spec = pl.BlockSpec((None, HALO, D_MODEL), lambda b, i: (b, 0, 0))
    in_specs = [halo_spec, tile, pl.BlockSpec((kpad, D_MODEL), lambda b, i: (0, 0))]
    args = [halo3, u3, taps]
    if mode == "A":
        in_specs += [row, row, row]
        args += [e.reshape(1, D_MODEL) for e in extras]
    else:
        in_specs += [tile]
        args += list(extras)
    return pl.pallas_call(
        functools.partial(_conv_kernel, mode, ktaps, tm, rc, zero_first),
        grid=(nb, t // tm),
        in_specs=in_specs,
        out_specs=tile,
        out_shape=jax.ShapeDtypeStruct((nb, t, D_MODEL), BF16),
        scratch_shapes=[pltpu.VMEM((HALO + tm, D_MODEL), F32)],
        compiler_params=_cparams("parallel", "arbitrary"),
        name="conv_" + mode,
    )(*args)


def _dil_attn_kernel(dil, q_ref, kp_ref, kc_ref, vp_ref, vc_ref, bias_ref, o_ref, lse_ref):
    bq = q_ref.shape[0] // dil
    heads_per_blk = q_ref.shape[1] // HEAD_DIM
    j = pl.program_id(1)
    i = pl.program_id(2)
    lane = lax.broadcasted_iota(jnp.int32, (1, LANES), 1)
    lo = lane < HEAD_DIM
    col = lax.broadcasted_iota(jnp.int32, (1, 2 * bq), 1)
    dead = col < jnp.where(i == 0, bq, 0)
    for r in range(dil):
        rows = pl.ds(r, bq, stride=dil) if dil > 1 else pl.ds(0, bq)
        for hp in range(heads_per_blk // 2):
            sl = slice(LANES * hp, LANES * (hp + 1))
            q = q_ref[rows, sl] * ATTN_SCALE
            k = jnp.concatenate([kp_ref[rows, sl], kc_ref[rows, sl]], axis=0).astype(BF16)
            v = jnp.concatenate([vp_ref[rows, sl], vc_ref[rows, sl]], axis=0).astype(BF16)
            outs, lses = [], []
            for e in range(2):
                qe = jnp.where(lo if e == 0 else jnp.logical_not(lo), q, 0.0).astype(BF16)
                s = _dot_nt(qe, k) + bias_ref[j * heads_per_blk + 2 * hp + e]
                s = jnp.where(dead, NEG_INF, s)
                m = jnp.max(s, axis=-1, keepdims=True)
                p = jnp.exp(s - m)
                l = jnp.sum(p, axis=-1, keepdims=True)
                outs.append(_dot(p.astype(BF16), v) / l)
                lses.append(m + jnp.log(l))
            o_ref[rows, sl] = jnp.where(lo, outs[0], outs[1])
            lse_ref[rows, sl] = jnp.where(lo, lses[0], lses[1])


def dilated_prompt_attn(qkv3, bias, g, dil):
    bn, s_len, ncols = qkv3.shape
    bq = 128
    tb = bq * dil
    lbw = D_MODEL if dil == 1 else LANES
    nlb = D_MODEL // lbw
    ng = len(B_PAIRS)

    def spec(which, prev):
        def imap(b, j, i):
            blk = jnp.maximum(i - 1, 0) if prev else i
            return (b, blk, (which * ng + g) * nlb + j)
        return pl.BlockSpec((None, tb, lbw), imap)

    out_spec = pl.BlockSpec((None, tb, lbw), lambda b, j, i: (b, i, j))
    o, lse = pl.pallas_call(
        functools.partial(_dil_attn_kernel, dil),
        grid=(bn, nlb, s_len // tb),
        in_specs=[spec(0, False), spec(1, True), spec(1, False), spec(2, True), spec(2, False),
                  pl.BlockSpec((N_HEADS, bq, 2 * bq), lambda b, j, i: (0, 0, 0))],
        out_specs=[out_spec, out_spec],
        out_shape=[jax.ShapeDtypeStruct((bn, s_len, D_MODEL), F32)] * 2,
        compiler_params=_cparams("parallel", "parallel", "arbitrary"),
        name="dilated_prompt",
    )(qkv3, qkv3, qkv3, qkv3, qkv3, bias)
    return o.reshape(bn * s_len, D_MODEL), lse.reshape(bn * s_len, D_MODEL)


def _head_mask(ncols):
    row = lax.broadcasted_iota(jnp.int32, (N_HEADS * DEC_T, ncols), 0)
    lane = lax.broadcasted_iota(jnp.int32, (N_HEADS * DEC_T, ncols), 1)
    return (row >> 3) == (lane >> 6)


def _build_qbd(q8):
    q = q8 * ATTN_SCALE
    qt = jnp.concatenate([q] * N_HEADS, axis=0)
    return jnp.where(_head_mask(D_MODEL), qt, 0.0).astype(BF16)


def _fold_heads(x):
    xm = jnp.where(_head_mask(D_MODEL), x, 0.0)
    out = xm[0:DEC_T, :]
    for h in range(1, N_HEADS):
        out = out + xm[h * DEC_T:(h + 1) * DEC_T, :]
    return out


def _decode_init(q_ref, new_ref, bnew_ref, qbd_scr, m_scr, l_scr, acc_scr):
    qbd = _build_qbd(q_ref[...])
    qbd_scr[...] = qbd
    kn = new_ref[:, 0:D_MODEL].astype(BF16)
    vn = new_ref[:, D_MODEL:2 * D_MODEL].astype(BF16)
    s = _dot_nt(qbd, kn) + bnew_ref[...]
    m = jnp.max(s, axis=-1, keepdims=True)
    p = jnp.exp(s - m)
    m_scr[...] = m
    l_scr[...] = jnp.sum(p, axis=-1, keepdims=True)
    acc_scr[...] = _dot(p.astype(BF16), vn)


def _decode_update(kt, vt, bias, qbd_scr, m_scr, l_scr, acc_scr):
    s = _dot(qbd_scr[...], kt) + bias
    m_old = m_scr[...]
    m_new = jnp.maximum(m_old, jnp.max(s, axis=-1, keepdims=True))
    alpha = jnp.exp(m_old - m_new)
    p = jnp.exp(s - m_new)
    l_scr[...] = alpha * l_scr[...] + jnp.sum(p, axis=-1, keepdims=True)
    acc_scr[...] = alpha * acc_scr[...] + _dot_nt(p.astype(BF16), vt)
    m_scr[...] = m_new


def _bdec_kernel(q_ref, new_ref, bnew_ref, buf_ref, bias_ref, o_ref, lse_ref,
                 qbd_scr, m_scr, l_scr, acc_scr):
    c = pl.program_id(1)

    @pl.when(c == 0)
    def _():
        _decode_init(q_ref, new_ref, bnew_ref, qbd_scr, m_scr, l_scr, acc_scr)

    _decode_update(buf_ref[0:D_MODEL, :].astype(BF16), buf_ref[D_MODEL:2 * D_MODEL, :].astype(BF16),
                   bias_ref[...], qbd_scr, m_scr, l_scr, acc_scr)

    @pl.when(c == pl.num_programs(1) - 1)
    def _():
        l = l_scr[...]
        o_ref[...] = _fold_heads(acc_scr[...] / l)
        lse_ref[...] = _fold_heads(jnp.broadcast_to(m_scr[...] + jnp.log(l), (N_HEADS * DEC_T, D_MODEL)))


def dilated_decode(q3, new3, bias_new, buf_t, bias_buf):
    bn, _, L = buf_t.shape
    rows = N_HEADS * DEC_T
    rchunk = min(L, 512)
    o, lse = pl.pallas_call(
        _bdec_kernel,
        grid=(bn, L // rchunk),
        in_specs=[
            pl.BlockSpec((None, DEC_T, D_MODEL), lambda b, c: (b, 0, 0)),
            pl.BlockSpec((None, NEW_PAD, 2 * D_MODEL), lambda b, c: (b, 0, 0)),
            pl.BlockSpec((rows, NEW_PAD), lambda b, c: (0, 0)),
            pl.BlockSpec((None, 2 * D_MODEL, rchunk), lambda b, c: (b, 0, c)),
            pl.BlockSpec((rows, rchunk), lambda b, c: (0, c)),
        ],
        out_specs=[pl.BlockSpec((None, DEC_T, D_MODEL), lambda b, c: (b, 0, 0))] * 2,
        out_shape=[jax.ShapeDtypeStruct((bn, DEC_T, D_MODEL), F32)] * 2,
        scratch_shapes=[pltpu.VMEM((rows, D_MODEL), BF16), pltpu.VMEM((rows, 1), F32),
                        pltpu.VMEM((rows, 1), F32), pltpu.VMEM((rows, D_MODEL), F32)],
        compiler_params=_cparams("parallel", "arbitrary"),
        name="dilated_decode",
    )(q3, new3, bias_new, buf_t, bias_buf)
    return o.reshape(bn * DEC_T, D_MODEL), lse.reshape(bn * DEC_T, D_MODEL)


def _fgate_kernel(seg_shift, use_carry, p_ref, b_ref, logf_ref, cum_ref, carry):
    tt = p_ref.shape[0]
    x = p_ref[...] + b_ref[...]
    lf = jnp.minimum(x, 0.0) - jnp.log(1.0 + jnp.exp(-jnp.abs(x)))
    logf_ref[...] = lf
    r = lax.broadcasted_iota(jnp.int32, (tt, tt), 0)
    c = lax.broadcasted_iota(jnp.int32, (tt, tt), 1)
    tri = jnp.where(c <= r, 1.0, 0.0)
    if seg_shift is not None:
        tri = jnp.where((r >> seg_shift) == (c >> seg_shift), tri, 0.0)
    cum = _dot_exact(tri, lf)
    if use_carry:
        @pl.when(pl.program_id(1) == 0)
        def _():
            carry[...] = jnp.zeros_like(carry)

        cum = cum + carry[...]
        carry[...] = cum[tt - 1:tt, :]
    cum_ref[...] = cum


def fgate(p3, b_f, tt, seg_shift, use_carry):
    bn, t, ncols = p3.shape
    fcol = (ncols - LANES) // LANES
    b_pad = jnp.pad(b_f.reshape(1, -1), ((0, 0), (0, LANES - b_f.shape[-1])))
    out_spec = pl.BlockSpec((None, tt, LANES), lambda b, i: (b, i, 0))
    return pl.pallas_call(
        functools.partial(_fgate_kernel, seg_shift, use_carry),
        grid=(bn, t // tt),
        in_specs=[pl.BlockSpec((None, tt, LANES), lambda b, i: (b, i, fcol)),
                  pl.BlockSpec((1, LANES), lambda b, i: (0, 0))],
        out_specs=[out_spec, out_spec],
        out_shape=[jax.ShapeDtypeStruct((bn, t, LANES), F32)] * 2,
        scratch_shapes=[pltpu.VMEM((1, LANES), F32)],
        compiler_params=_cparams("parallel", "arbitrary"),
        name="fgate",
    )(p3, b_pad)


def _split3(x):
    hi = x.astype(BF16)
    r1 = x - hi.astype(F32)
    mid = r1.astype(BF16)
    lo = (r1 - mid.astype(F32)).astype(BF16)
    return jnp.concatenate([hi, mid, lo], axis=1)


def _piece_selector(h, first_lane, sign):
    rho = lax.broadcasted_iota(jnp.int32, (3 * LANES, LANES), 0)
    lam = lax.broadcasted_iota(jnp.int32, (3 * LANES, LANES), 1)
    at_target = jnp.where(lam == first_lane + (rho >> 7), sign, 0.0)
    return jnp.where((rho & (LANES - 1)) == h, at_target, 0.0).astype(BF16)


def _fox_flash_kernel(tq, q_ref, k_ref, v_ref, cq_ref, ck_ref, o_ref, kaug, vaug, m_scr, acc_scr):
    hp = pl.program_id(1)
    qi = pl.program_id(2)
    lane = lax.broadcasted_iota(jnp.int32, (1, LANES), 1)
    lo = lane < HEAD_DIM
    own = (lo, jnp.logical_not(lo))
    base = (HEAD_DIM, 0)

    def ones_at(first):
        return jnp.where(lane >= first, jnp.where(lane < first + 3, 1.0, 0.0), 0.0)

    @pl.when(qi == 0)
    def _():
        pieces = _split3(ck_ref[...] * LOG2E)
        for e in range(2):
            extras = _dot(pieces, _piece_selector(2 * hp + e, base[e], -1.0)) + ones_at(base[e] + 3)
            kaug[e] = jnp.where(own[e], k_ref[...], extras).astype(BF16)
            vaug[e] = jnp.where(own[e], v_ref[...], 1.0).astype(BF16)

    qpieces = _split3(cq_ref[...] * LOG2E)
    qs = q_ref[...] * (ATTN_SCALE * LOG2E)
    qaug = []
    for e in range(2):
        extras = _dot(qpieces, _piece_selector(2 * hp + e, base[e] + 3, 1.0)) + ones_at(base[e])
        qaug.append(jnp.where(own[e], qs, extras).astype(BF16))
    m_scr[...] = jnp.full(m_scr.shape, NEG_INF, F32)
    acc_scr[...] = jnp.zeros_like(acc_scr)

    def tile(kt, diagonal):
        ks = pl.multiple_of(kt * tq, tq)
        for e in range(2):
            s = _dot_nt(qaug[e], kaug[e, pl.ds(ks, tq), :])
            if diagonal:
                r = lax.broadcasted_iota(jnp.int32, (tq, tq), 0)
                c = lax.broadcasted_iota(jnp.int32, (tq, tq), 1)
                s = jnp.where(r >= c, s, NEG_INF)
            m_old = m_scr[e]
            m_new = jnp.maximum(m_old, jnp.max(s, axis=-1, keepdims=True))
            alpha = jnp.exp2(m_old - m_new)
            p = jnp.exp2(s - jnp.concatenate([m_new] * (tq // LANES), axis=1))
            acc_scr[e] = alpha * acc_scr[e] + _dot(p.astype(BF16), vaug[e, pl.ds(ks, tq), :])
            m_scr[e] = m_new

    def body(kt, carry):
        tile(kt, False)
        return carry

    lax.fori_loop(0, qi, body, 0)
    tile(qi, True)
    a0, a1 = acc_scr[0], acc_scr[1]
    num = jnp.where(lo, a0, a1)
    den = jnp.where(lo, a1, a0)
    den = jnp.concatenate([den[:, HEAD_DIM:], den[:, :HEAD_DIM]], axis=1)
    o_ref[...] = (num / den).astype(BF16)


def fox_flash(p3, cum3):
    bn, s_len, _ = p3.shape
    tq = 512
    nh2 = N_HEADS // 2
    return pl.pallas_call(
        functools.partial(_fox_flash_kernel, tq),
        grid=(bn, nh2, s_len // tq),
        in_specs=[
            pl.BlockSpec((None, tq, LANES), lambda b, h, i: (b, i, h)),
            pl.BlockSpec((None, s_len, LANES), lambda b, h, i: (b, 0, nh2 + h)),
            pl.BlockSpec((None, s_len, LANES), lambda b, h, i: (b, 0, 2 * nh2 + h)),
            pl.BlockSpec((None, tq, LANES), lambda b, h, i: (b, i, 0)),
            pl.BlockSpec((None, s_len, LANES), lambda b, h, i: (b, 0, 0)),
        ],
        out_specs=pl.BlockSpec((None, tq, LANES), lambda b, h, i: (b, i, h)),
        out_shape=jax.ShapeDtypeStruct((bn, s_len, D_MODEL), BF16),
        scratch_shapes=[pltpu.VMEM((2, s_len, LANES), BF16), pltpu.VMEM((2, s_len, LANES), BF16),
                        pltpu.VMEM((2, tq, LANES), F32), pltpu.VMEM((2, tq, LANES), F32)],
        compiler_params=_cparams("parallel", "parallel", "arbitrary"),
        name="fox_flash",
    )(p3, p3, p3, cum3, cum3)


def _foxdec_kernel(npg, pt_ref, q_ref, new_ref, cncol_ref, bnew_ref, *rest):
    kv_refs = rest[:npg]
    lf_refs = rest[npg:2 * npg]
    o_ref, qbd_scr, m_scr, l_scr, acc_scr, carry, lfpad, kcat, vcat = rest[2 * npg:]
    j = pl.program_id(1)
    rows = N_HEADS * DEC_T

    @pl.when(j == 0)
    def _():
        _decode_init(q_ref, new_ref, bnew_ref, qbd_scr, m_scr, l_scr, acc_scr)
        carry[...] = jnp.zeros_like(carry)
        lfpad[...] = jnp.zeros_like(lfpad)

    r = lax.broadcasted_iota(jnp.int32, (rows, PAGE), 0)
    c = lax.broadcasted_iota(jnp.int32, (rows, PAGE), 1)
    expand = jnp.where((r >> 3) == c, 1.0, 0.0)
    later = jnp.where(r > c, 1.0, 0.0)
    after = carry[...]
    biases = []
    for i in range(npg):
        lfpad[0:N_HEADS, :] = lf_refs[i][...]
        lt = _dot_exact(expand, lfpad[...])
        biases.append(cncol_ref[...] + after + _dot_exact(lt, later))
        after = after + jnp.sum(lt, axis=-1, keepdims=True)
        kcat[:, i * PAGE:(i + 1) * PAGE] = kv_refs[i][0:D_MODEL, :].astype(BF16)
        vcat[:, i * PAGE:(i + 1) * PAGE] = kv_refs[i][D_MODEL:2 * D_MODEL, :].astype(BF16)
    carry[...] = after
    _decode_update(kcat[...], vcat[...], jnp.concatenate(biases, axis=1), qbd_scr, m_scr, l_scr, acc_scr)

    @pl.when(j == pl.num_programs(1) - 1)
    def _():
        o_ref[...] = _fold_heads(acc_scr[...] / l_scr[...])


def fox_decode(page_table, q3, new3, cn_col, bias_new, cache_kvt, cache_lft, u):
    bn, n_pages = page_table.shape
    npg = 4
    rows = N_HEADS * DEC_T

    def page_spec(i, nrows):
        def imap(b, j, pt):
            return (u, pt[b, n_pages - 1 - (j * npg + i)], 0, 0)
        return pl.BlockSpec((None, None, nrows, PAGE), imap)

    per_b = lambda shape: pl.BlockSpec((None,) + shape, lambda b, j, pt: (b, 0, 0))
    grid_spec = pltpu.PrefetchScalarGridSpec(
        num_scalar_prefetch=1,
        grid=(bn, n_pages // npg),
        in_specs=[per_b((DEC_T, D_MODEL)), per_b((NEW_PAD, 2 * D_MODEL)), per_b((rows, 1)),
                  per_b((rows, NEW_PAD))]
                 + [page_spec(i, 2 * D_MODEL) for i in range(npg)]
                 + [page_spec(i, N_HEADS) for i in range(npg)],
        out_specs=per_b((DEC_T, D_MODEL)),
        scratch_shapes=[pltpu.VMEM((rows, D_MODEL), BF16), pltpu.VMEM((rows, 1), F32),
                        pltpu.VMEM((rows, 1), F32), pltpu.VMEM((rows, D_MODEL), F32),
                        pltpu.VMEM((rows, 1), F32), pltpu.VMEM((PAGE, PAGE), F32),
                        pltpu.VMEM((D_MODEL, npg * PAGE), BF16), pltpu.VMEM((D_MODEL, npg * PAGE), BF16)],
    )
    o = pl.pallas_call(
        functools.partial(_foxdec_kernel, npg),
        grid_spec=grid_spec,
        out_shape=jax.ShapeDtypeStruct((bn, DEC_T, D_MODEL), F32),
        compiler_params=_cparams("parallel", "arbitrary"),
        name="fox_decode",
    )(page_table, q3, new3, cn_col, bias_new, *([cache_kvt] * npg), *([cache_lft] * npg))
    return o.reshape(bn * DEC_T, D_MODEL)


def _t5_bucket(dist):
    max_exact = REL_BUCKETS // 2
    df = jnp.maximum(dist, 1).astype(F32)
    large = max_exact + (jnp.log(df / max_exact) / math.log(REL_MAX_DIST / max_exact)
                         * (REL_BUCKETS - max_exact)).astype(jnp.int32)
    large = jnp.minimum(large, REL_BUCKETS - 1)
    return jnp.where(dist < max_exact, dist, large)


def _bias_table(tab, dist, valid):
    onehot = (_t5_bucket(dist)[..., None] == jnp.arange(REL_BUCKETS)).astype(F32)
    b = jnp.einsum("qkb,bh->hqk", onehot, tab.astype(F32), precision=lax.Precision.HIGHEST)
    return jnp.where(valid[None], b, NEG_INF)


def _prompt_bias(tab, dil, ns, bq):
    steps = jnp.arange(bq)[:, None] - jnp.arange(2 * bq)[None, :] + ns
    valid = (steps >= 0) & (steps <= ns)
    return _bias_table(tab, jnp.clip(steps, 0, ns) * dil, valid)


def _decode_bias(tab, dil, ns, lb):
    t = jnp.arange(DEC_T)

    def table(dist):
        valid = (dist >= 0) & (dist % dil == 0) & (dist <= ns * dil)
        return _bias_table(tab, jnp.maximum(dist, 0), valid).reshape(N_HEADS * DEC_T, dist.shape[1])

    d_buf = lb + t[:, None] - jnp.arange(lb)[None, :]
    s = jnp.arange(NEW_PAD)
    d_new = jnp.where(s[None, :] < DEC_T, t[:, None] - s[None, :], -1)
    return table(d_buf), table(d_new)


def _bcol(which, g):
    c0 = (which * len(B_PAIRS) + g) * D_MODEL
    return slice(c0, c0 + D_MODEL)


def dilated_prompt_stage(qkv3, rel_bias):
    bn, s_len, _ = qkv3.shape
    outs, lses, states = [], [], []
    for g, (win, dil) in enumerate(B_PAIRS):
        ns = win // dil
        assert ns == 128 and s_len % (128 * dil) == 0
        tab = rel_bias[:, g * N_HEADS:(g + 1) * N_HEADS]
        o, lse = dilated_prompt_attn(qkv3, _prompt_bias(tab, dil, ns, 128), g, dil)
        outs.append(o)
        lses.append(lse)
        lw = min(win, s_len)
        states.append(jnp.stack([qkv3[:, s_len - lw:, _bcol(1, g)], qkv3[:, s_len - lw:, _bcol(2, g)]],
                                axis=2).reshape(bn, lw, 2, N_HEADS, HEAD_DIM))
    return outs, lses, states


def dilated_sample_stage(qkv3, bufs, rel_bias):
    dn, t_new, _ = qkv3.shape
    outs, lses, states = [], [], []
    for g, (win, dil) in enumerate(B_PAIRS):
        ns = win // dil
        tab = rel_bias[:, g * N_HEADS:(g + 1) * N_HEADS]
        lb = bufs[g].shape[1]
        assert lb == ns * dil
        buf_t = bufs[g].transpose(0, 2, 3, 4, 1).reshape(dn, 2 * D_MODEL, lb)
        new_kv = jnp.concatenate([qkv3[:, :, _bcol(1, g)], qkv3[:, :, _bcol(2, g)]], axis=-1)
        bias_buf, bias_new = _decode_bias(tab, dil, ns, lb)
        o, lse = dilated_decode(qkv3[:, :, _bcol(0, g)],
                                jnp.pad(new_kv, ((0, 0), (0, NEW_PAD - t_new), (0, 0))),
                                bias_new, buf_t, bias_buf)
        outs.append(o)
        lses.append(lse)
        new_t = jnp.concatenate([buf_t[:, :, t_new:], new_kv.transpose(0, 2, 1)], axis=2)
        states.append(new_t.reshape(dn, 2, N_HEADS, HEAD_DIM, lb).transpose(0, 4, 1, 2, 3))
    return outs, lses, states


def fox_prompt_stage(p3, b_f):
    bn, s_len, _ = p3.shape
    nf = b_f.shape[-1]
    logf, cum = fgate(p3, b_f, 512, None, True)
    o = fox_flash(p3, cum)
    kv_pages = p3[:, :, D_MODEL:3 * D_MODEL].reshape(bn, s_len // PAGE, PAGE, 2, N_HEADS, HEAD_DIM)
    return o, kv_pages, logf[:, :, :nf].reshape(bn, s_len // PAGE, PAGE, nf)


def fox_sample_stage(p3, b_f, page_table, cache_c_kv, cache_c_logf, u):
    dn, t_new, _ = p3.shape
    nf = b_f.shape[-1]
    logf, cn = fgate(p3.reshape(1, dn * t_new, -1), b_f, dn * t_new, 3, False)
    cn_t = cn[0, :, :N_HEADS].reshape(dn, t_new, N_HEADS).transpose(0, 2, 1)
    cn_col = cn_t.reshape(dn, N_HEADS * t_new, 1)
    cn_keys = jnp.broadcast_to(cn_t[:, :, None, :], (dn, N_HEADS, t_new, t_new))
    cn_keys = cn_keys.reshape(dn, N_HEADS * t_new, t_new)
    tq_idx = jnp.tile(jnp.arange(t_new), N_HEADS)[:, None]
    causal = jnp.arange(t_new)[None, :] <= tq_idx
    bias_new = jnp.where(causal[None], cn_col - cn_keys, NEG_INF)
    bias_new = jnp.pad(bias_new, ((0, 0), (0, 0), (0, NEW_PAD - t_new)), constant_values=NEG_INF)
    new_kv = p3[:, :, D_MODEL:3 * D_MODEL]
    nc, n_phys = cache_c_kv.shape[:2]
    kvt = cache_c_kv.transpose(0, 1, 3, 4, 5, 2).reshape(nc, n_phys, 2 * D_MODEL, PAGE)
    lft = cache_c_logf.transpose(0, 1, 3, 2)
    o = fox_decode(page_table, p3[:, :, :D_MODEL],
                   jnp.pad(new_kv, ((0, 0), (0, NEW_PAD - t_new), (0, 0))), cn_col, bias_new, kvt, lft, u)
    return (o, new_kv.reshape(dn, t_new, 2, N_HEADS, HEAD_DIM),
            logf[0, :, :nf].reshape(dn, t_new, nf))


def kernel(x_prompt, x_sample, c_prompt, c_sample, state_a_conv, state_b_kv_w128, state_b_kv_w512, state_b_kv_w2048, cache_c_kv, cache_c_logf, page_table, state_d_conv, ada_w, ada_b, norm_g, final_g, ffn_w13, ffn_w2, a_w_in, a_b_in, a_w_dw, a_b_dw, a_ln_g, a_ln_b, a_w_out, a_b_out, b_w_qkv, b_w_out, rel_bias, c_w_in, c_b_f, c_w_out, d_w_in, d_w_conv, d_w_out):
    bn, s_len, _ = x_prompt.shape
    dn, t_new, _ = x_sample.shape
    assert t_new == DEC_T and s_len % 512 == 0
    n_p, n_s = bn * s_len, dn * t_new
    depth = ada_w.shape[0]
    tm_p, tm_s = 1024, n_s
    b_states = (state_b_kv_w128, state_b_kv_w512, state_b_kv_w2048)

    mod = ada_mod(jnp.concatenate([c_prompt, c_sample], axis=0), ada_w, ada_b)
    mod = mod.reshape(depth, bn + dn, 9, D_MODEL)

    xp = x_prompt.reshape(n_p, D_MODEL)
    xs = x_sample.reshape(n_s, D_MODEL)
    outs = {}
    for l in range(depth):
        kind, u = l % 4, l // 4
        mod_p = mod[l, :bn].reshape(bn, 9, 1, D_MODEL)
        mod_s = jnp.repeat(mod[l, bn:], t_new, axis=0).transpose(1, 0, 2)[None]
        paths = ((xp, mod_p, tm_p), (xs, mod_s, tm_s))

        w13, w2 = ffn_w13[l, 0].astype(BF16), ffn_w2[l, 0].astype(BF16)
        xp, xs = [ffn(x, m, 0, norm_g[l, 0], w13, w2, tm) for x, m, tm in paths]
        paths = ((xp, mod_p, tm_p), (xs, mod_s, tm_s))

        if kind == 0:
            w_in, w_out = a_w_in[u].astype(BF16), a_w_out[u].astype(BF16)
            tn = 512
            us = [nmm(x, m, 1, norm_g[l, 1], w_in, (0, D_MODEL // tn), D_MODEL // tn, tn, tm,
                      bias=a_b_in[u], mode="glu") for x, m, tm in paths]
            up3, us3 = us[0].reshape(bn, s_len, D_MODEL), us[1].reshape(dn, t_new, D_MODEL)
            extras = (a_b_dw[u], a_ln_g[u], a_ln_b[u])
            yp = conv_mix("A", up3, up3, a_w_dw[u], 256, True, extras)
            halo_s = jnp.pad(state_a_conv[u], ((0, 0), (HALO - (CONV_A - 1), 0), (0, 0)))
            ys = conv_mix("A", us3, halo_s, a_w_dw[u], t_new, False, extras)
            xp = mmr([yp.reshape(n_p, D_MODEL)], xp, mod_p, 1, w_out, tm_p, bias=a_b_out[u])
            xs = mmr([ys.reshape(n_s, D_MODEL)], xs, mod_s, 1, w_out, tm_s, bias=a_b_out[u])
            outs.setdefault("a_p", []).append(up3[:, s_len - (CONV_A - 1):])
            outs.setdefault("a_s", []).append(
                jnp.concatenate([state_a_conv[u], us3], axis=1)[:, t_new:])
        elif kind == 1:
            w_qkv, w_out = b_w_qkv[u].astype(BF16), b_w_out[u].astype(BF16)
            ncol = w_qkv.shape[1] // 1024
            qkv_p, qkv_s = [nmm(x, m, 1, norm_g[l, 1], w_qkv, (0,), ncol, 1024, tm)
                            for x, m, tm in paths]
            po, pl_, pst = dilated_prompt_stage(qkv_p.reshape(bn, s_len, -1), rel_bias)
            so, sl_, sst = dilated_sample_stage(qkv_s.reshape(dn, t_new, -1),
                                                [b[u] for b in b_states], rel_bias)
            xp = mmr(po + pl_, xp, mod_p, 1, w_out, tm_p // 2)
            xs = mmr(so + sl_, xs, mod_s, 1, w_out, tm_s)
            for g in range(len(B_PAIRS)):
                outs.setdefault("b%d_p" % g, []).append(pst[g])
                outs.setdefault("b%d_s" % g, []).append(sst[g])
        elif kind == 2:
            nf = c_b_f.shape[-1]
            w_in = jnp.pad(c_w_in[u], ((0, 0), (0, LANES - nf))).astype(BF16)
            w_out = c_w_out[u].astype(BF16)
            tn = 640
            ncol = w_in.shape[1] // tn
            pp, ps = [nmm(x, m, 1, norm_g[l, 1], w_in, (0,), ncol, tn, tm) for x, m, tm in paths]
            o_p, kv_p, lf_p = fox_prompt_stage(pp.reshape(bn, s_len, -1), c_b_f[u])
            o_s, kv_s, lf_s = fox_sample_stage(ps.reshape(dn, t_new, -1), c_b_f[u], page_table,
                                               cache_c_kv, cache_c_logf, u)
            xp = mmr([o_p.reshape(n_p, D_MODEL)], xp, mod_p, 1, w_out, tm_p)
            xs = mmr([o_s], xs, mod_s, 1, w_out, tm_s)
            outs.setdefault("ckv_p", []).append(kv_p)
            outs.setdefault("clf_p", []).append(lf_p)
            outs.setdefault("ckv_s", []).append(kv_s)
            outs.setdefault("clf_s", []).append(lf_s)
        else:
            w_in, w_out = d_w_in[u].astype(BF16), d_w_out[u].astype(BF16)
            tn = 512
            nc = D_MODEL // tn
            (bg_p, z_p), (bg_s, z_s) = [
                nmm(x, m, 1, norm_g[l, 1], w_in, (0, nc, 2 * nc), nc, tn, tm, mode="gate3")
                for x, m, tm in paths]
            zp3, zs3 = z_p.reshape(bn, s_len, D_MODEL), z_s.reshape(dn, t_new, D_MODEL)
            yp = conv_mix("D", zp3, zp3, d_w_conv[u], 256, True, (bg_p.reshape(bn, s_len, D_MODEL),))
            halo_s = jnp.pad(state_d_conv[u], ((0, 0), (HALO - (CONV_D - 1), 0), (0, 0)))
            ys = conv_mix("D", zs3, halo_s, d_w_conv[u], t_new, False, (bg_s.reshape(dn, t_new, D_MODEL),))
            xp = mmr([yp.reshape(n_p, D_MODEL)], xp, mod_p, 1, w_out, tm_p)
            xs = mmr([ys.reshape(n_s, D_MODEL)], xs, mod_s, 1, w_out, tm_s)
            outs.setdefault("d_p", []).append(zp3[:, s_len - (CONV_D - 1):])
            outs.setdefault("d_s", []).append(
                jnp.concatenate([state_d_conv[u], zs3], axis=1)[:, t_new:])

        paths = ((xp, mod_p, tm_p), (xs, mod_s, tm_s))
        w13, w2 = ffn_w13[l, 1].astype(BF16), ffn_w2[l, 1].astype(BF16)
        fg = final_g if l == depth - 1 else None
        xp, xs = [ffn(x, m, 2, norm_g[l, 2], w13, w2, tm, final_g=fg) for x, m, tm in paths]

    st = lambda key: jnp.stack(outs[key])
    return (xp.reshape(bn, s_len, D_MODEL), xs.reshape(dn, t_new, D_MODEL),
            st("a_p"), st("a_s"), st("b0_p"), st("b0_s"), st("b1_p"), st("b1_s"), st("b2_p"), st("b2_s"),
            st("ckv_p"), st("clf_p"), st("ckv_s"), st("clf_s"), st("d_p"), st("d_s"))
```

```python
import functools
import math

import jax
import jax.numpy as jnp
from jax import lax
from jax.experimental import pallas as pl
from jax.experimental.pallas import tpu as pltpu

F32 = jnp.float32
BF16 = jnp.bfloat16

D_MODEL = 1024
D_FF = 2816
NORM_EPS = 1e-6
HEAD_DIM = 64
N_HEADS = 16
ATTN_SCALE = HEAD_DIM ** -0.5
LOG2E = math.log2(math.e)
NEG_INF = -1e30
CONV_A = 31
CONV_D = 3
B_PAIRS = ((128, 1), (512, 4), (2048, 16))
REL_BUCKETS = 32
REL_MAX_DIST = 2048
PAGE = 128
DEC_T = 8

LANES = 128
HALO = 32
NEW_PAD = 16
VMEM_LIMIT = 56 * 1024 * 1024

NT_DIMS = (((1,), (1,)), ((), ()))


def _cparams(*sem):
    return pltpu.CompilerParams(dimension_semantics=sem, vmem_limit_bytes=VMEM_LIMIT)


def _dot(a, b):
    return jnp.dot(a, b, preferred_element_type=F32)


def _dot_nt(a, b):
    return lax.dot_general(a, b, NT_DIMS, preferred_element_type=F32)


def _dot_exact(a, b):
    return jnp.dot(a, b, preferred_element_type=F32, precision=lax.Precision.HIGHEST)


def _norm_mod(x, g, shift, scale):
    y = x * lax.rsqrt(jnp.mean(x * x, axis=-1, keepdims=True) + NORM_EPS) * g
    return y * (1.0 + scale) + shift


def _ada_kernel(c_ref, w_ref, b_ref, o_ref):
    c = c_ref[...]
    h = (c * jax.nn.sigmoid(c)).astype(BF16)
    o_ref[...] = _dot(h, w_ref[...].astype(BF16)) + b_ref[...]


def ada_mod(c_all, ada_w, ada_b):
    depth, _, n = ada_w.shape
    nb = c_all.shape[0]
    tn = 1024
    return pl.pallas_call(
        _ada_kernel,
        grid=(depth, n // tn),
        in_specs=[
            pl.BlockSpec((nb, D_MODEL), lambda l, j: (0, 0)),
            pl.BlockSpec((None, D_MODEL, tn), lambda l, j: (l, 0, j)),
            pl.BlockSpec((None, 1, tn), lambda l, j: (l, 0, j)),
        ],
        out_specs=pl.BlockSpec((None, nb, tn), lambda l, j: (l, 0, j)),
        out_shape=jax.ShapeDtypeStruct((depth, nb, n), F32),
        compiler_params=_cparams("parallel", "parallel"),
        name="ada_mod",
    )(c_all, ada_w, ada_b.reshape(depth, 1, n))


FFN_CHUNK = 512


def _ffn_kernel(has_final, x_ref, g_ref, sh_ref, sc_ref, gt_ref, w13_ref, w2_ref, *rest):
    if has_final:
        fg_ref, o_ref = rest
    else:
        (o_ref,) = rest
    x = x_ref[...]
    h = _norm_mod(x, g_ref[...], sh_ref[...], sc_ref[...]).astype(BF16)
    acc = None
    for c0 in range(0, D_FF, FFN_CHUNK):
        c1 = min(c0 + FFN_CHUNK, D_FF)
        g = _dot(h, w13_ref[:, c0:c1])
        up = _dot(h, w13_ref[:, D_FF + c0:D_FF + c1])
        a = (g * jax.nn.sigmoid(g) * up).astype(BF16)
        part = _dot(a, w2_ref[c0:c1, :])
        acc = part if acc is None else acc + part
    y = x + (0.5 * gt_ref[...]) * acc
    if has_final:
        y = y * lax.rsqrt(jnp.mean(y * y, axis=-1, keepdims=True) + NORM_EPS) * fg_ref[...]
    o_ref[...] = y


def ffn(x, mod4, j, g_row, w13_all, w2_all, layer, which, tm, final_g=None):
    n = x.shape[0]
    tpb = (n // mod4.shape[0]) // tm
    r = mod4.shape[2]
    mspec = lambda idx: pl.BlockSpec((None, None, r, D_MODEL), lambda i: (i // tpb, idx, 0, 0))
    in_specs = [
        pl.BlockSpec((tm, D_MODEL), lambda i: (i, 0)),
        pl.BlockSpec((1, D_MODEL), lambda i: (0, 0)),
        mspec(3 * j), mspec(3 * j + 1), mspec(3 * j + 2),
        pl.BlockSpec((None, None, D_MODEL, 2 * D_FF), lambda i: (layer, which, 0, 0)),
        pl.BlockSpec((None, None, D_FF, D_MODEL), lambda i: (layer, which, 0, 0)),
    ]
    args = [x, g_row.reshape(1, D_MODEL), mod4, mod4, mod4, w13_all, w2_all]
    if final_g is not None:
        in_specs.append(pl.BlockSpec((1, D_MODEL), lambda i: (0, 0)))
        args.append(final_g.reshape(1, D_MODEL))
    return pl.pallas_call(
        functools.partial(_ffn_kernel, final_g is not None),
        grid=(n // tm,),
        in_specs=in_specs,
        out_specs=pl.BlockSpec((tm, D_MODEL), lambda i: (i, 0)),
        out_shape=jax.ShapeDtypeStruct((n, D_MODEL), F32),
        compiler_params=_cparams("parallel"),
        name="ffn",
    )(*args)


def _nmm_kernel(mode, nw, has_bias, x_ref, g_ref, sh_ref, sc_ref, *rest):
    w_refs = rest[:nw]
    rest = rest[nw:]
    b_refs = rest[:nw] if has_bias else ()
    rest = rest[len(b_refs):]
    nout = 2 if mode == "gate3" else 1
    o_refs = rest[:nout]
    h_scr = rest[nout]

    @pl.when(pl.program_id(1) == 0)
    def _():
        h_scr[...] = _norm_mod(x_ref[...], g_ref[...], sh_ref[...], sc_ref[...]).astype(BF16)

    h = h_scr[...]
    ps = [_dot(h, w[...]) for w in w_refs]
    if has_bias:
        ps = [p + b[...] for p, b in zip(ps, b_refs)]
    if mode == "plain":
        o_refs[0][...] = ps[0]
    elif mode == "glu":
        o_refs[0][...] = ps[0] * jax.nn.sigmoid(ps[1])
    else:
        o_refs[0][...] = ps[0]
        o_refs[1][...] = ps[1] * ps[2]


def nmm(x, mod4, j, g_row, w, col_offsets, ncol, tn, tm, bias=None, mode="plain"):
    n = x.shape[0]
    nw = len(col_offsets)
    tpb = (n // mod4.shape[0]) // tm
    r = mod4.shape[2]
    mspec = lambda idx: pl.BlockSpec((None, None, r, D_MODEL), lambda i, c: (i // tpb, idx, 0, 0))
    in_specs = [
        pl.BlockSpec((tm, D_MODEL), lambda i, c: (i, 0)),
        pl.BlockSpec((1, D_MODEL), lambda i, c: (0, 0)),
        mspec(3 * j), mspec(3 * j + 1),
    ]
    args = [x, g_row.reshape(1, D_MODEL), mod4, mod4]
    for off in col_offsets:
        in_specs.append(pl.BlockSpec((D_MODEL, tn), lambda i, c, off=off: (0, off + c)))
        args.append(w)
    if bias is not None:
        b2 = bias.reshape(1, -1)
        for off in col_offsets:
            in_specs.append(pl.BlockSpec((1, tn), lambda i, c, off=off: (0, off + c)))
            args.append(b2)
    nout = 2 if mode == "gate3" else 1
    out_specs = [pl.BlockSpec((tm, tn), lambda i, c: (i, c)) for _ in range(nout)]
    out_shape = [jax.ShapeDtypeStruct((n, ncol * tn), F32) for _ in range(nout)]
    outs = pl.pallas_call(
        functools.partial(_nmm_kernel, mode, nw, bias is not None),
        grid=(n // tm, ncol),
        in_specs=in_specs,
        out_specs=out_specs,
        out_shape=out_shape,
        scratch_shapes=[pltpu.VMEM((tm, D_MODEL), BF16)],
        compiler_params=_cparams("parallel", "arbitrary"),
        name="nmm_" + mode,
    )(*args)
    return outs if nout == 2 else outs[0]


def _mmr_kernel(merge, has_bias, *refs):
    if merge:
        o1, o2, o3, l1, l2, l3 = refs[:6]
        refs = refs[6:]
        a1, a2, a3 = l1[...], l2[...], l3[...]
        m = jnp.maximum(jnp.maximum(a1, a2), a3)
        e1, e2, e3 = jnp.exp(a1 - m), jnp.exp(a2 - m), jnp.exp(a3 - m)
        inv = 1.0 / (e1 + e2 + e3)
        a = ((e1 * inv) * o1[...] + (e2 * inv) * o2[...] + (e3 * inv) * o3[...]).astype(BF16)
    else:
        a = refs[0][...].astype(BF16)
        refs = refs[1:]
    if has_bias:
        x_ref, gt_ref, w_ref, b_ref, o_ref = refs
    else:
        x_ref, gt_ref, w_ref, o_ref = refs
    y = _dot(a, w_ref[...])
    if has_bias:
        y = y + b_ref[...]
    o_ref[...] = x_ref[...] + gt_ref[...] * y


def mmr(a_list, x, mod4, j, w, tm, bias=None):
    n = x.shape[0]
    merge = len(a_list) == 6
    tpb = (n // mod4.shape[0]) // tm
    r = mod4.shape[2]
    tile = pl.BlockSpec((tm, D_MODEL), lambda i: (i, 0))
    in_specs = [tile for _ in a_list] + [
        tile,
        pl.BlockSpec((None, None, r, D_MODEL), lambda i: (i // tpb, 3 * j + 2, 0, 0)),
        pl.BlockSpec((D_MODEL, D_MODEL), lambda i: (0, 0)),
    ]
    args = list(a_list) + [x, mod4, w]
    if bias is not None:
        in_specs.append(pl.BlockSpec((1, D_MODEL), lambda i: (0, 0)))
        args.append(bias.reshape(1, D_MODEL))
    return pl.pallas_call(
        functools.partial(_mmr_kernel, merge, bias is not None),
        grid=(n // tm,),
        in_specs=in_specs,
        out_specs=tile,
        out_shape=jax.ShapeDtypeStruct((n, D_MODEL), F32),
        compiler_params=_cparams("parallel"),
        name="mmr_merge" if merge else "mmr",
    )(*args)


SUBLANES = 8


def _conv_shift_slots(ktaps):
    base = HALO - (ktaps - 1)
    shifts = sorted({(base + k) % SUBLANES for k in range(ktaps)} - {0})
    return {b: slot for slot, b in enumerate(shifts)}


def _conv_kernel(mode, ktaps, tm, rc, zero_first, halo_ref, u_ref, w_ref, *rest):
    if mode == "A":
        bdw_ref, lng_ref, lnb_ref, o_ref, cat, shifted = rest
    else:
        bg_ref, o_ref, cat, shifted = rest
    if zero_first:
        first = pl.program_id(1) == 0

        @pl.when(first)
        def _():
            cat[0:HALO, :] = jnp.zeros((HALO, D_MODEL), F32)

        @pl.when(jnp.logical_not(first))
        def _():
            cat[0:HALO, :] = halo_ref[...]
    else:
        cat[0:HALO, :] = halo_ref[...]
    cat[HALO:HALO + tm, :] = u_ref[...]
    base = HALO - (ktaps - 1)
    slots = _conv_shift_slots(ktaps)
    span = tm + HALO - SUBLANES
    for b, slot in slots.items():
        shifted[slot] = cat[b:b + span, :]
    for r0 in range(0, tm, rc):
        acc = None
        for k in range(ktaps):
            b = (base + k) % SUBLANES
            row = r0 + (base + k) - b
            src = cat[row:row + rc, :] if b == 0 else shifted[slots[b], row:row + rc, :]
            term = src * w_ref[k:k + 1, :]
            acc = term if acc is None else acc + term
        if mode == "A":
            y = acc + bdw_ref[...]
            mu = jnp.mean(y, axis=-1, keepdims=True)
            yc = y - mu
            var = jnp.mean(yc * yc, axis=-1, keepdims=True)
            yn = yc * lax.rsqrt(var + NORM_EPS) * lng_ref[...] + lnb_ref[...]
            o_ref[r0:r0 + rc, :] = (yn * jax.nn.sigmoid(yn)).astype(BF16)
        else:
            o_ref[r0:r0 + rc, :] = (bg_ref[r0:r0 + rc, :] * acc).astype(BF16)


def conv_mix(mode, u3, halo3, taps, tm, zero_first, extras):
    nb, t, _ = u3.shape
    ktaps = taps.shape[0]
    kpad = -(-ktaps // 8) * 8
    taps = jnp.pad(taps, ((0, kpad - ktaps), (0, 0)))
    rc = min(32, tm)
    hb = tm // HALO
    tile = pl.BlockSpec((None, tm, D_MODEL), lambda b, i: (b, i, 0))
    row = pl.BlockSpec((1, D_MODEL), lambda b, i: (0, 0))
    if zero_first:
        halo_spec = pl.BlockSpec((None, HALO, D_MODEL), lambda b, i: (b, jnp.maximum(i * hb - 1, 0), 0))
    else:
        halo_spec = pl.BlockSpec((None, HALO, D_MODEL), lambda b, i: (b, 0, 0))
    in_specs = [halo_spec, tile, pl.BlockSpec((kpad, D_MODEL), lambda b, i: (0, 0))]
    args = [halo3, u3, taps]
    if mode == "A":
        in_specs += [row, row, row]
        args += [e.reshape(1, D_MODEL) for e in extras]
    else:
        in_specs += [tile]
        args += list(extras)
    return pl.pallas_call(
        functools.partial(_conv_kernel, mode, ktaps, tm, rc, zero_first),
        grid=(nb, t // tm),
        in_specs=in_specs,
        out_specs=tile,
        out_shape=jax.ShapeDtypeStruct((nb, t, D_MODEL), BF16),
        scratch_shapes=[pltpu.VMEM((HALO + tm, D_MODEL), F32),
                        pltpu.VMEM((len(_conv_shift_slots(ktaps)), tm + HALO - SUBLANES, D_MODEL), F32)],
        compiler_params=_cparams("parallel", "arbitrary"),
        name="conv_" + mode,
    )(*args)


def _dil_attn_kernel(dil, q_ref, kp_ref, kc_ref, vp_ref, vc_ref, bias_ref, o_ref, lse_ref):
    bq = q_ref.shape[0] // dil
    heads_per_blk = q_ref.shape[1] // HEAD_DIM
    j = pl.program_id(1)
    i = pl.program_id(2)
    lane = lax.broadcasted_iota(jnp.int32, (1, LANES), 1)
    lo = lane < HEAD_DIM
    col = lax.broadcasted_iota(jnp.int32, (1, 2 * bq), 1)
    dead = col < jnp.where(i == 0, bq, 0)
    for r in range(dil):
        rows = pl.ds(r, bq, stride=dil) if dil > 1 else pl.ds(0, bq)
        for hp in range(heads_per_blk // 2):
            sl = slice(LANES * hp, LANES * (hp + 1))
            q = q_ref[rows, sl] * ATTN_SCALE
            k = jnp.concatenate([kp_ref[rows, sl], kc_ref[rows, sl]], axis=0).astype(BF16)
            v = jnp.concatenate([vp_ref[rows, sl], vc_ref[rows, sl]], axis=0).astype(BF16)
            outs, lses = [], []
            for e in range(2):
                qe = jnp.where(lo if e == 0 else jnp.logical_not(lo), q, 0.0).astype(BF16)
                s = _dot_nt(qe, k) + bias_ref[j * heads_per_blk + 2 * hp + e]
                s = jnp.where(dead, NEG_INF, s)
                m = jnp.max(s, axis=-1, keepdims=True)
                p = jnp.exp(s - m)
                l = jnp.sum(p, axis=-1, keepdims=True)
                outs.append(_dot(p.astype(BF16), v) / l)
                lses.append(m + jnp.log(l))
            o_ref[rows, sl] = jnp.where(lo, outs[0], outs[1])
            lse_ref[rows, sl] = jnp.where(lo, lses[0], lses[1])


def dilated_prompt_attn(qkv3, bias, g, dil):
    bn, s_len, ncols = qkv3.shape
    bq = 128
    tb = bq * dil
    lbw = D_MODEL if dil == 1 else LANES
    nlb = D_MODEL // lbw
    ng = len(B_PAIRS)

    def spec(which, prev):
        def imap(b, j, i):
            blk = jnp.maximum(i - 1, 0) if prev else i
            return (b, blk, (which * ng + g) * nlb + j)
        return pl.BlockSpec((None, tb, lbw), imap)

    out_spec = pl.BlockSpec((None, tb, lbw), lambda b, j, i: (b, i, j))
    o, lse = pl.pallas_call(
        functools.partial(_dil_attn_kernel, dil),
        grid=(bn, nlb, s_len // tb),
        in_specs=[spec(0, False), spec(1, True), spec(1, False), spec(2, True), spec(2, False),
                  pl.BlockSpec((N_HEADS, bq, 2 * bq), lambda b, j, i: (0, 0, 0))],
        out_specs=[out_spec, out_spec],
        out_shape=[jax.ShapeDtypeStruct((bn, s_len, D_MODEL), F32)] * 2,
        compiler_params=_cparams("parallel", "parallel", "arbitrary"),
        name="dilated_prompt",
    )(qkv3, qkv3, qkv3, qkv3, qkv3, bias)
    return o.reshape(bn * s_len, D_MODEL), lse.reshape(bn * s_len, D_MODEL)


def _head_mask(ncols):
    row = lax.broadcasted_iota(jnp.int32, (N_HEADS * DEC_T, ncols), 0)
    lane = lax.broadcasted_iota(jnp.int32, (N_HEADS * DEC_T, ncols), 1)
    return (row >> 3) == (lane >> 6)


def _build_qbd(q8):
    q = q8 * ATTN_SCALE
    qt = jnp.concatenate([q] * N_HEADS, axis=0)
    return jnp.where(_head_mask(D_MODEL), qt, 0.0).astype(BF16)


def _fold_heads(x):
    xm = jnp.where(_head_mask(D_MODEL), x, 0.0)
    out = xm[0:DEC_T, :]
    for h in range(1, N_HEADS):
        out = out + xm[h * DEC_T:(h + 1) * DEC_T, :]
    return out


def _decode_init(q_ref, new_ref, bnew_ref, qbd_scr, m_scr, l_scr, acc_scr):
    qbd = _build_qbd(q_ref[...])
    qbd_scr[...] = qbd
    kn = new_ref[:, 0:D_MODEL].astype(BF16)
    vn = new_ref[:, D_MODEL:2 * D_MODEL].astype(BF16)
    s = _dot_nt(qbd, kn) + bnew_ref[...]
    m = jnp.max(s, axis=-1, keepdims=True)
    p = jnp.exp(s - m)
    m_scr[...] = m
    l_scr[...] = jnp.sum(p, axis=-1, keepdims=True)
    acc_scr[...] = _dot(p.astype(BF16), vn)


def _decode_update(kt, vt, bias, qbd_scr, m_scr, l_scr, acc_scr):
    s = _dot(qbd_scr[...], kt) + bias
    m_old = m_scr[...]
    m_new = jnp.maximum(m_old, jnp.max(s, axis=-1, keepdims=True))
    alpha = jnp.exp(m_old - m_new)
    p = jnp.exp(s - m_new)
    l_scr[...] = alpha * l_scr[...] + jnp.sum(p, axis=-1, keepdims=True)
    acc_scr[...] = alpha * acc_scr[...] + _dot_nt(p.astype(BF16), vt)
    m_scr[...] = m_new


def _bdec_kernel(q_ref, new_ref, bnew_ref, buf_ref, nxt_ref, newt_ref, bias_ref, o_ref, lse_ref, st_ref,
                 qbd_scr, m_scr, l_scr, acc_scr):
    c = pl.program_id(1)
    last = c == pl.num_programs(1) - 1

    @pl.when(c == 0)
    def _():
        _decode_init(q_ref, new_ref, bnew_ref, qbd_scr, m_scr, l_scr, acc_scr)

    _decode_update(buf_ref[0:D_MODEL, :].astype(BF16), buf_ref[D_MODEL:2 * D_MODEL, :].astype(BF16),
                   bias_ref[...], qbd_scr, m_scr, l_scr, acc_scr)

    width = buf_ref.shape[1]
    rows_blk = 256
    for r0 in range(0, 2 * D_MODEL, rows_blk):
        rs = slice(r0, r0 + rows_blk)
        tail = jnp.where(last, newt_ref[rs, :], nxt_ref[rs, :])
        x = jnp.concatenate([buf_ref[rs, :], tail], axis=1)
        st_ref[rs, :] = x[:, DEC_T:DEC_T + width]

    @pl.when(last)
    def _():
        l = l_scr[...]
        o_ref[...] = _fold_heads(acc_scr[...] / l)
        lse_ref[...] = _fold_heads(jnp.broadcast_to(m_scr[...] + jnp.log(l), (N_HEADS * DEC_T, D_MODEL)))


def dilated_decode(q3, new3, new_t, bias_new, buf_t, bias_buf):
    bn, _, L = buf_t.shape
    rows = N_HEADS * DEC_T
    rchunk = min(L, 512)
    lane_blocks = rchunk // LANES
    per_b = lambda shape: pl.BlockSpec((None,) + shape, lambda b, c: (b, 0, 0))
    o, lse, st = pl.pallas_call(
        _bdec_kernel,
        grid=(bn, L // rchunk),
        in_specs=[
            per_b((DEC_T, D_MODEL)),
            per_b((NEW_PAD, 2 * D_MODEL)),
            pl.BlockSpec((rows, NEW_PAD), lambda b, c: (0, 0)),
            pl.BlockSpec((None, 2 * D_MODEL, rchunk), lambda b, c: (b, 0, c)),
            pl.BlockSpec((None, 2 * D_MODEL, LANES),
                         lambda b, c: (b, 0, jnp.minimum((c + 1) * lane_blocks, L // LANES - 1))),
            per_b((2 * D_MODEL, LANES)),
            pl.BlockSpec((rows, rchunk), lambda b, c: (0, c)),
        ],
        out_specs=[per_b((DEC_T, D_MODEL)), per_b((DEC_T, D_MODEL)),
                   pl.BlockSpec((None, 2 * D_MODEL, rchunk), lambda b, c: (b, 0, c))],
        out_shape=[jax.ShapeDtypeStruct((bn, DEC_T, D_MODEL), F32)] * 2
                  + [jax.ShapeDtypeStruct((bn, 2 * D_MODEL, L), F32)],
        scratch_shapes=[pltpu.VMEM((rows, D_MODEL), BF16), pltpu.VMEM((rows, 1), F32),
                        pltpu.VMEM((rows, 1), F32), pltpu.VMEM((rows, D_MODEL), F32)],
        compiler_params=_cparams("parallel", "arbitrary"),
        name="dilated_decode",
    )(q3, new3, bias_new, buf_t, buf_t, new_t, bias_buf)
    return o.reshape(bn * DEC_T, D_MODEL), lse.reshape(bn * DEC_T, D_MODEL), st


def _fgate_kernel(seg_shift, use_carry, p_ref, b_ref, logf_ref, cum_ref, carry):
    tt = p_ref.shape[0]
    x = p_ref[...] + b_ref[...]
    lf = jnp.minimum(x, 0.0) - jnp.log(1.0 + jnp.exp(-jnp.abs(x)))
    logf_ref[...] = lf
    r = lax.broadcasted_iota(jnp.int32, (tt, tt), 0)
    c = lax.broadcasted_iota(jnp.int32, (tt, tt), 1)
    tri = jnp.where(c <= r, 1.0, 0.0)
    if seg_shift is not None:
        tri = jnp.where((r >> seg_shift) == (c >> seg_shift), tri, 0.0)
    cum = _dot_exact(tri, lf)
    if use_carry:
        @pl.when(pl.program_id(1) == 0)
        def _():
            carry[...] = jnp.zeros_like(carry)

        cum = cum + carry[...]
        carry[...] = cum[tt - 1:tt, :]
    cum_ref[...] = cum


def fgate(p3, b_f, tt, seg_shift, use_carry):
    bn, t, ncols = p3.shape
    fcol = (ncols - LANES) // LANES
    b_pad = jnp.pad(b_f.reshape(1, -1), ((0, 0), (0, LANES - b_f.shape[-1])))
    out_spec = pl.BlockSpec((None, tt, LANES), lambda b, i: (b, i, 0))
    return pl.pallas_call(
        functools.partial(_fgate_kernel, seg_shift, use_carry),
        grid=(bn, t // tt),
        in_specs=[pl.BlockSpec((None, tt, LANES), lambda b, i: (b, i, fcol)),
                  pl.BlockSpec((1, LANES), lambda b, i: (0, 0))],
        out_specs=[out_spec, out_spec],
        out_shape=[jax.ShapeDtypeStruct((bn, t, LANES), F32)] * 2,
        scratch_shapes=[pltpu.VMEM((1, LANES), F32)],
        compiler_params=_cparams("parallel", "arbitrary"),
        name="fgate",
    )(p3, b_pad)


def _split3(x):
    hi = x.astype(BF16)
    r1 = x - hi.astype(F32)
    mid = r1.astype(BF16)
    lo = (r1 - mid.astype(F32)).astype(BF16)
    return jnp.concatenate([hi, mid, lo], axis=1)


def _piece_selector(h, first_lane, sign):
    rho = lax.broadcasted_iota(jnp.int32, (3 * LANES, LANES), 0)
    lam = lax.broadcasted_iota(jnp.int32, (3 * LANES, LANES), 1)
    at_target = jnp.where(lam == first_lane + (rho >> 7), sign, 0.0)
    return jnp.where((rho & (LANES - 1)) == h, at_target, 0.0).astype(BF16)


def _fox_flash_kernel(tq, q_ref, k_ref, v_ref, cq_ref, ck_ref, o_ref, kvt_ref, kaug, vaug, m_scr, acc_scr):
    hp = pl.program_id(1)
    qi = pl.program_id(2)
    lane = lax.broadcasted_iota(jnp.int32, (1, LANES), 1)
    lo = lane < HEAD_DIM
    own = (lo, jnp.logical_not(lo))
    base = (HEAD_DIM, 0)

    def ones_at(first):
        return jnp.where(lane >= first, jnp.where(lane < first + 3, 1.0, 0.0), 0.0)

    @pl.when(qi == 0)
    def _():
        pieces = _split3(ck_ref[...] * LOG2E)
        for e in range(2):
            extras = _dot(pieces, _piece_selector(2 * hp + e, base[e], -1.0)) + ones_at(base[e] + 3)
            kaug[e] = jnp.where(own[e], k_ref[...], extras).astype(BF16)
            vaug[e] = jnp.where(own[e], v_ref[...], 1.0).astype(BF16)
        for pg in range(kvt_ref.shape[0]):
            rs = slice(pg * PAGE, (pg + 1) * PAGE)
            kvt_ref[pg, 0] = k_ref[rs, :].T
            kvt_ref[pg, 1] = v_ref[rs, :].T

    qpieces = _split3(cq_ref[...] * LOG2E)
    qs = q_ref[...] * (ATTN_SCALE * LOG2E)
    qaug = []
    for e in range(2):
        extras = _dot(qpieces, _piece_selector(2 * hp + e, base[e] + 3, 1.0)) + ones_at(base[e])
        qaug.append(jnp.where(own[e], qs, extras).astype(BF16))
    m_scr[...] = jnp.full(m_scr.shape, NEG_INF, F32)
    acc_scr[...] = jnp.zeros_like(acc_scr)

    def tile(kt, diagonal):
        ks = pl.multiple_of(kt * tq, tq)
        for e in range(2):
            s = _dot_nt(qaug[e], kaug[e, pl.ds(ks, tq), :])
            if diagonal:
                r = lax.broadcasted_iota(jnp.int32, (tq, tq), 0)
                c = lax.broadcasted_iota(jnp.int32, (tq, tq), 1)
                s = jnp.where(r >= c, s, NEG_INF)
            m_old = m_scr[e]
            m_new = jnp.maximum(m_old, jnp.max(s, axis=-1, keepdims=True))
            alpha = jnp.exp2(m_old - m_new)
            p = jnp.exp2(s - jnp.concatenate([m_new] * (tq // LANES), axis=1))
            acc_scr[e] = alpha * acc_scr[e] + _dot(p.astype(BF16), vaug[e, pl.ds(ks, tq), :])
            m_scr[e] = m_new

    def body(kt, carry):
        tile(kt, False)
        return carry

    lax.fori_loop(0, qi, body, 0)
    tile(qi, True)
    a0, a1 = acc_scr[0], acc_scr[1]
    num = jnp.where(lo, a0, a1)
    den = jnp.where(lo, a1, a0)
    den = jnp.concatenate([den[:, HEAD_DIM:], den[:, :HEAD_DIM]], axis=1)
    o_ref[...] = (num / den).astype(BF16)


def fox_flash(p3, cum3):
    bn, s_len, _ = p3.shape
    tq = 512
    nh2 = N_HEADS // 2
    n_pg = s_len // PAGE
    return pl.pallas_call(
        functools.partial(_fox_flash_kernel, tq),
        grid=(bn, nh2, s_len // tq),
        in_specs=[
            pl.BlockSpec((None, tq, LANES), lambda b, h, i: (b, i, h)),
            pl.BlockSpec((None, s_len, LANES), lambda b, h, i: (b, 0, nh2 + h)),
            pl.BlockSpec((None, s_len, LANES), lambda b, h, i: (b, 0, 2 * nh2 + h)),
            pl.BlockSpec((None, tq, LANES), lambda b, h, i: (b, i, 0)),
            pl.BlockSpec((None, s_len, LANES), lambda b, h, i: (b, 0, 0)),
        ],
        out_specs=[pl.BlockSpec((None, tq, LANES), lambda b, h, i: (b, i, h)),
                   pl.BlockSpec((None, n_pg, 2, None, LANES, PAGE), lambda b, h, i: (b, 0, 0, h, 0, 0))],
        out_shape=[jax.ShapeDtypeStruct((bn, s_len, D_MODEL), BF16),
                   jax.ShapeDtypeStruct((bn, n_pg, 2, nh2, LANES, PAGE), F32)],
        scratch_shapes=[pltpu.VMEM((2, s_len, LANES), BF16), pltpu.VMEM((2, s_len, LANES), BF16),
                        pltpu.VMEM((2, tq, LANES), F32), pltpu.VMEM((2, tq, LANES), F32)],
        compiler_params=_cparams("parallel", "parallel", "arbitrary"),
        name="fox_flash",
    )(p3, p3, p3, cum3, cum3)


def _suffix_sums(x):
    lane = lax.broadcasted_iota(jnp.int32, x.shape, 1)
    y = x
    step = 1
    while step < x.shape[1]:
        ahead = pltpu.roll(y, x.shape[1] - step, axis=1)
        y = y + jnp.where(lane < x.shape[1] - step, ahead, 0.0)
        step *= 2
    return y


def _rows_per_head(x):
    return jnp.concatenate([jnp.broadcast_to(x[h:h + 1, :], (DEC_T, x.shape[1])) for h in range(N_HEADS)],
                           axis=0)


def _foxdec_kernel(npg, pt_ref, q_ref, new_ref, cncol_ref, bnew_ref, *rest):
    kv_refs = rest[:npg]
    lf_refs = rest[npg:2 * npg]
    o_ref, qbd_scr, m_scr, l_scr, acc_scr, carry, kcat, vcat = rest[2 * npg:]
    j = pl.program_id(1)

    @pl.when(j == 0)
    def _():
        _decode_init(q_ref, new_ref, bnew_ref, qbd_scr, m_scr, l_scr, acc_scr)
        carry[...] = jnp.zeros_like(carry)

    after = carry[...]
    biases = []
    for i in range(npg):
        lf = lf_refs[i][...]
        incl = _suffix_sums(lf)
        biases.append(after + (incl - lf))
        after = after + incl[:, 0:1]
        kcat[:, i * PAGE:(i + 1) * PAGE] = kv_refs[i][0:D_MODEL, :].astype(BF16)
        vcat[:, i * PAGE:(i + 1) * PAGE] = kv_refs[i][D_MODEL:2 * D_MODEL, :].astype(BF16)
    carry[...] = after
    bias = _rows_per_head(jnp.concatenate(biases, axis=1)) + cncol_ref[...]
    _decode_update(kcat[...], vcat[...], bias, qbd_scr, m_scr, l_scr, acc_scr)

    @pl.when(j == pl.num_programs(1) - 1)
    def _():
        o_ref[...] = _fold_heads(acc_scr[...] / l_scr[...])


def fox_decode(page_table, q3, new3, cn_col, bias_new, cache_kvt, cache_lft, u):
    bn, n_pages = page_table.shape
    npg = 8
    assert n_pages % npg == 0
    rows = N_HEADS * DEC_T

    def page_spec(i, nrows):
        def imap(b, j, pt):
            return (u, pt[b, n_pages - 1 - (j * npg + i)], 0, 0)
        return pl.BlockSpec((None, None, nrows, PAGE), imap)

    per_b = lambda shape: pl.BlockSpec((None,) + shape, lambda b, j, pt: (b, 0, 0))
    grid_spec = pltpu.PrefetchScalarGridSpec(
        num_scalar_prefetch=1,
        grid=(bn, n_pages // npg),
        in_specs=[per_b((DEC_T, D_MODEL)), per_b((NEW_PAD, 2 * D_MODEL)), per_b((rows, 1)),
                  per_b((rows, NEW_PAD))]
                 + [page_spec(i, 2 * D_MODEL) for i in range(npg)]
                 + [page_spec(i, N_HEADS) for i in range(npg)],
        out_specs=per_b((DEC_T, D_MODEL)),
        scratch_shapes=[pltpu.VMEM((rows, D_MODEL), BF16), pltpu.VMEM((rows, 1), F32),
                        pltpu.VMEM((rows, 1), F32), pltpu.VMEM((rows, D_MODEL), F32),
                        pltpu.VMEM((N_HEADS, PAGE), F32),
                        pltpu.VMEM((D_MODEL, npg * PAGE), BF16), pltpu.VMEM((D_MODEL, npg * PAGE), BF16)],
    )
    o = pl.pallas_call(
        functools.partial(_foxdec_kernel, npg),
        grid_spec=grid_spec,
        out_shape=jax.ShapeDtypeStruct((bn, DEC_T, D_MODEL), F32),
        compiler_params=_cparams("parallel", "arbitrary"),
        name="fox_decode",
    )(page_table, q3, new3, cn_col, bias_new, *([cache_kvt] * npg), *([cache_lft] * npg))
    return o.reshape(bn * DEC_T, D_MODEL)


def _t5_bucket(dist):
    max_exact = REL_BUCKETS // 2
    df = jnp.maximum(dist, 1).astype(F32)
    large = max_exact + (jnp.log(df / max_exact) / math.log(REL_MAX_DIST / max_exact)
                         * (REL_BUCKETS - max_exact)).astype(jnp.int32)
    large = jnp.minimum(large, REL_BUCKETS - 1)
    return jnp.where(dist < max_exact, dist, large)


def _bias_table(tab, dist, valid):
    onehot = (_t5_bucket(dist)[..., None] == jnp.arange(REL_BUCKETS)).astype(F32)
    b = jnp.einsum("qkb,bh->hqk", onehot, tab.astype(F32), precision=lax.Precision.HIGHEST)
    return jnp.where(valid[None], b, NEG_INF)


def _prompt_bias(tab, dil, ns, bq):
    steps = jnp.arange(bq)[:, None] - jnp.arange(2 * bq)[None, :] + ns
    valid = (steps >= 0) & (steps <= ns)
    return _bias_table(tab, jnp.clip(steps, 0, ns) * dil, valid)


def _decode_bias(tab, dil, ns, lb):
    t = jnp.arange(DEC_T)

    def table(dist):
        valid = (dist >= 0) & (dist % dil == 0) & (dist <= ns * dil)
        return _bias_table(tab, jnp.maximum(dist, 0), valid).reshape(N_HEADS * DEC_T, dist.shape[1])

    d_buf = lb + t[:, None] - jnp.arange(lb)[None, :]
    s = jnp.arange(NEW_PAD)
    d_new = jnp.where(s[None, :] < DEC_T, t[:, None] - s[None, :], -1)
    return table(d_buf), table(d_new)


def _bcol(which, g):
    c0 = (which * len(B_PAIRS) + g) * D_MODEL
    return slice(c0, c0 + D_MODEL)


def dilated_prompt_stage(qkv3, rel_bias):
    bn, s_len, _ = qkv3.shape
    outs, lses, states = [], [], []
    for g, (win, dil) in enumerate(B_PAIRS):
        ns = win // dil
        assert ns == 128 and s_len % (128 * dil) == 0
        tab = rel_bias[:, g * N_HEADS:(g + 1) * N_HEADS]
        o, lse = dilated_prompt_attn(qkv3, _prompt_bias(tab, dil, ns, 128), g, dil)
        outs.append(o)
        lses.append(lse)
        lw = min(win, s_len)
        states.append(jnp.stack([qkv3[:, s_len - lw:, _bcol(1, g)], qkv3[:, s_len - lw:, _bcol(2, g)]],
                                axis=2).reshape(bn, lw, 2, N_HEADS, HEAD_DIM))
    return outs, lses, states


def dilated_sample_stage(qkv3, bufs, rel_bias):
    dn, t_new, _ = qkv3.shape
    outs, lses, states = [], [], []
    for g, (win, dil) in enumerate(B_PAIRS):
        ns = win // dil
        tab = rel_bias[:, g * N_HEADS:(g + 1) * N_HEADS]
        lb = bufs[g].shape[1]
        assert lb == ns * dil
        buf_t = bufs[g].transpose(0, 2, 3, 4, 1).reshape(dn, 2 * D_MODEL, lb)
        new_kv = jnp.concatenate([qkv3[:, :, _bcol(1, g)], qkv3[:, :, _bcol(2, g)]], axis=-1)
        bias_buf, bias_new = _decode_bias(tab, dil, ns, lb)
        new_t = jnp.pad(new_kv.transpose(0, 2, 1), ((0, 0), (0, 0), (0, LANES - t_new)))
        o, lse, st = dilated_decode(qkv3[:, :, _bcol(0, g)],
                                    jnp.pad(new_kv, ((0, 0), (0, NEW_PAD - t_new), (0, 0))),
                                    new_t, bias_new, buf_t, bias_buf)
        outs.append(o)
        lses.append(lse)
        states.append(st.reshape(dn, 2, N_HEADS, HEAD_DIM, lb).transpose(0, 4, 1, 2, 3))
    return outs, lses, states


def fox_prompt_stage(p3, b_f):
    bn, s_len, _ = p3.shape
    nf = b_f.shape[-1]
    logf, cum = fgate(p3, b_f, 512, None, True)
    o, kvt = fox_flash(p3, cum)
    kv_pages = kvt.reshape(bn, s_len // PAGE, 2, N_HEADS, HEAD_DIM, PAGE).transpose(0, 1, 5, 2, 3, 4)
    return o, kv_pages, logf[:, :, :nf].reshape(bn, s_len // PAGE, PAGE, nf)


def fox_sample_stage(p3, b_f, page_table, cache_c_kv, cache_c_logf, u):
    dn, t_new, _ = p3.shape
    nf = b_f.shape[-1]
    logf, cn = fgate(p3.reshape(1, dn * t_new, -1), b_f, dn * t_new, 3, False)
    cn_t = cn[0, :, :N_HEADS].reshape(dn, t_new, N_HEADS).transpose(0, 2, 1)
    cn_col = cn_t.reshape(dn, N_HEADS * t_new, 1)
    cn_keys = jnp.broadcast_to(cn_t[:, :, None, :], (dn, N_HEADS, t_new, t_new))
    cn_keys = cn_keys.reshape(dn, N_HEADS * t_new, t_new)
    tq_idx = jnp.tile(jnp.arange(t_new), N_HEADS)[:, None]
    causal = jnp.arange(t_new)[None, :] <= tq_idx
    bias_new = jnp.where(causal[None], cn_col - cn_keys, NEG_INF)
    bias_new = jnp.pad(bias_new, ((0, 0), (0, 0), (0, NEW_PAD - t_new)), constant_values=NEG_INF)
    new_kv = p3[:, :, D_MODEL:3 * D_MODEL]
    nc, n_phys = cache_c_kv.shape[:2]
    kvt = cache_c_kv.transpose(0, 1, 3, 4, 5, 2).reshape(nc, n_phys, 2 * D_MODEL, PAGE)
    lft = cache_c_logf.transpose(0, 1, 3, 2)
    o = fox_decode(page_table, p3[:, :, :D_MODEL],
                   jnp.pad(new_kv, ((0, 0), (0, NEW_PAD - t_new), (0, 0))), cn_col, bias_new, kvt, lft, u)
    return (o, new_kv.reshape(dn, t_new, 2, N_HEADS, HEAD_DIM),
            logf[0, :, :nf].reshape(dn, t_new, nf))


def kernel(x_prompt, x_sample, c_prompt, c_sample, state_a_conv, state_b_kv_w128, state_b_kv_w512, state_b_kv_w2048, cache_c_kv, cache_c_logf, page_table, state_d_conv, ada_w, ada_b, norm_g, final_g, ffn_w13, ffn_w2, a_w_in, a_b_in, a_w_dw, a_b_dw, a_ln_g, a_ln_b, a_w_out, a_b_out, b_w_qkv, b_w_out, rel_bias, c_w_in, c_b_f, c_w_out, d_w_in, d_w_conv, d_w_out):
    bn, s_len, _ = x_prompt.shape
    dn, t_new, _ = x_sample.shape
    assert t_new == DEC_T and s_len % 512 == 0
    n_p, n_s = bn * s_len, dn * t_new
    depth = ada_w.shape[0]
    tm_p, tm_s = 1024, n_s
    tm_ffn = 512
    b_states = (state_b_kv_w128, state_b_kv_w512, state_b_kv_w2048)
    w13_all, w2_all = ffn_w13.astype(BF16), ffn_w2.astype(BF16)

    mod = ada_mod(jnp.concatenate([c_prompt, c_sample], axis=0), ada_w, ada_b)
    mod = mod.reshape(depth, bn + dn, 9, D_MODEL)

    xp = x_prompt.reshape(n_p, D_MODEL)
    xs = x_sample.reshape(n_s, D_MODEL)
    outs = {}
    for l in range(depth):
        kind, u = l % 4, l // 4
        mod_p = mod[l, :bn].reshape(bn, 9, 1, D_MODEL)
        mod_s = jnp.repeat(mod[l, bn:], t_new, axis=0).transpose(1, 0, 2)[None]
        ffn_paths = lambda xp, xs: ((xp, mod_p, tm_ffn), (xs, mod_s, tm_s))

        xp, xs = [ffn(x, m, 0, norm_g[l, 0], w13_all, w2_all, l, 0, tm) for x, m, tm in ffn_paths(xp, xs)]
        paths = ((xp, mod_p, tm_p), (xs, mod_s, tm_s))

        if kind == 0:
            w_in, w_out = a_w_in[u].astype(BF16), a_w_out[u].astype(BF16)
            tn = 512
            us = [nmm(x, m, 1, norm_g[l, 1], w_in, (0, D_MODEL // tn), D_MODEL // tn, tn, tm,
                      bias=a_b_in[u], mode="glu") for x, m, tm in paths]
            up3, us3 = us[0].reshape(bn, s_len, D_MODEL), us[1].reshape(dn, t_new, D_MODEL)
            extras = (a_b_dw[u], a_ln_g[u], a_ln_b[u])
            yp = conv_mix("A", up3, up3, a_w_dw[u], 256, True, extras)
            halo_s = jnp.pad(state_a_conv[u], ((0, 0), (HALO - (CONV_A - 1), 0), (0, 0)))
            ys = conv_mix("A", us3, halo_s, a_w_dw[u], t_new, False, extras)
            xp = mmr([yp.reshape(n_p, D_MODEL)], xp, mod_p, 1, w_out, tm_p, bias=a_b_out[u])
            xs = mmr([ys.reshape(n_s, D_MODEL)], xs, mod_s, 1, w_out, tm_s, bias=a_b_out[u])
            outs.setdefault("a_p", []).append(up3[:, s_len - (CONV_A - 1):])
            outs.setdefault("a_s", []).append(
                jnp.concatenate([state_a_conv[u], us3], axis=1)[:, t_new:])
        elif kind == 1:
            w_qkv, w_out = b_w_qkv[u].astype(BF16), b_w_out[u].astype(BF16)
            ncol = w_qkv.shape[1] // 1024
            qkv_p, qkv_s = [nmm(x, m, 1, norm_g[l, 1], w_qkv, (0,), ncol, 1024, tm)
                            for x, m, tm in paths]
            po, pl_, pst = dilated_prompt_stage(qkv_p.reshape(bn, s_len, -1), rel_bias)
            so, sl_, sst = dilated_sample_stage(qkv_s.reshape(dn, t_new, -1),
                                                [b[u] for b in b_states], rel_bias)
            xp = mmr(po + pl_, xp, mod_p, 1, w_out, tm_p // 2)
            xs = mmr(so + sl_, xs, mod_s, 1, w_out, tm_s)
            for g in range(len(B_PAIRS)):
                outs.setdefault("b%d_p" % g, []).append(pst[g])
                outs.setdefault("b%d_s" % g, []).append(sst[g])
        elif kind == 2:
            nf = c_b_f.shape[-1]
            w_in = jnp.pad(c_w_in[u], ((0, 0), (0, LANES - nf))).astype(BF16)
            w_out = c_w_out[u].astype(BF16)
            tn = 640
            ncol = w_in.shape[1] // tn
            pp, ps = [nmm(x, m, 1, norm_g[l, 1], w_in, (0,), ncol, tn, tm) for x, m, tm in paths]
            o_p, kv_p, lf_p = fox_prompt_stage(pp.reshape(bn, s_len, -1), c_b_f[u])
            o_s, kv_s, lf_s = fox_sample_stage(ps.reshape(dn, t_new, -1), c_b_f[u], page_table,
                                               cache_c_kv, cache_c_logf, u)
            xp = mmr([o_p.reshape(n_p, D_MODEL)], xp, mod_p, 1, w_out, tm_p)
            xs = mmr([o_s], xs, mod_s, 1, w_out, tm_s)
            outs.setdefault("ckv_p", []).append(kv_p)
            outs.setdefault("clf_p", []).append(lf_p)
            outs.setdefault("ckv_s", []).append(kv_s)
            outs.setdefault("clf_s", []).append(lf_s)
        else:
            w_in, w_out = d_w_in[u].astype(BF16), d_w_out[u].astype(BF16)
            tn = 512
            nc = D_MODEL // tn
            (bg_p, z_p), (bg_s, z_s) = [
                nmm(x, m, 1, norm_g[l, 1], w_in, (0, nc, 2 * nc), nc, tn, tm, mode="gate3")
                for x, m, tm in paths]
            zp3, zs3 = z_p.reshape(bn, s_len, D_MODEL), z_s.reshape(dn, t_new, D_MODEL)
            yp = conv_mix("D", zp3, zp3, d_w_conv[u], 256, True, (bg_p.reshape(bn, s_len, D_MODEL),))
            halo_s = jnp.pad(state_d_conv[u], ((0, 0), (HALO - (CONV_D - 1), 0), (0, 0)))
            ys = conv_mix("D", zs3, halo_s, d_w_conv[u], t_new, False, (bg_s.reshape(dn, t_new, D_MODEL),))
            xp = mmr([yp.reshape(n_p, D_MODEL)], xp, mod_p, 1, w_out, tm_p)
            xs = mmr([ys.reshape(n_s, D_MODEL)], xs, mod_s, 1, w_out, tm_s)
            outs.setdefault("d_p", []).append(zp3[:, s_len - (CONV_D - 1):])
            outs.setdefault("d_s", []).append(
                jnp.concatenate([state_d_conv[u], zs3], axis=1)[:, t_new:])

        fg = final_g if l == depth - 1 else None
        xp, xs = [ffn(x, m, 2, norm_g[l, 2], w13_all, w2_all, l, 1, tm, final_g=fg)
                  for x, m, tm in ffn_paths(xp, xs)]

    st = lambda key: jnp.stack(outs[key])
    return (xp.reshape(bn, s_len, D_MODEL), xs.reshape(dn, t_new, D_MODEL),
            st("a_p"), st("a_s"), st("b0_p"), st("b0_s"), st("b1_p"), st("b1_s"), st("b2_p"), st("b2_s"),
            st("ckv_p"), st("clf_p"), st("ckv_s"), st("clf_s"), st("d_p"), st("d_s"))
```

```python
import functools
import math

import jax
import jax.numpy as jnp
from jax import lax
from jax.experimental import pallas as pl
from jax.experimental.pallas import tpu as pltpu

F32 = jnp.float32
BF16 = jnp.bfloat16

D_MODEL = 1024
D_FF = 2816
NORM_EPS = 1e-6
HEAD_DIM = 64
N_HEADS = 16
ATTN_SCALE = HEAD_DIM ** -0.5
LOG2E = math.log2(math.e)
NEG_INF = -1e30
CONV_A = 31
CONV_D = 3
B_PAIRS = ((128, 1), (512, 4), (2048, 16))
REL_BUCKETS = 32
REL_MAX_DIST = 2048
PAGE = 128
DEC_T = 8

LANES = 128
HALO = 32
NEW_PAD = 16
VMEM_LIMIT = 56 * 1024 * 1024

NT_DIMS = (((1,), (1,)), ((), ()))


def _cparams(*sem):
    return pltpu.CompilerParams(dimension_semantics=sem, vmem_limit_bytes=VMEM_LIMIT)


def _dot(a, b):
    return jnp.dot(a, b, preferred_element_type=F32)


def _dot_nt(a, b):
    return lax.dot_general(a, b, NT_DIMS, preferred_element_type=F32)


def _dot_exact(a, b):
    return jnp.dot(a, b, preferred_element_type=F32, precision=lax.Precision.HIGHEST)


def _norm_mod(x, g, shift, scale):
    y = x * lax.rsqrt(jnp.mean(x * x, axis=-1, keepdims=True) + NORM_EPS) * g
    return y * (1.0 + scale) + shift


def _ada_kernel(c_ref, w_ref, b_ref, o_ref):
    c = c_ref[...]
    h = (c * jax.nn.sigmoid(c)).astype(BF16)
    o_ref[...] = _dot(h, w_ref[...].astype(BF16)) + b_ref[...]


def ada_mod(c_all, ada_w, ada_b):
    depth, _, n = ada_w.shape
    nb = c_all.shape[0]
    tn = 1024
    return pl.pallas_call(
        _ada_kernel,
        grid=(depth, n // tn),
        in_specs=[
            pl.BlockSpec((nb, D_MODEL), lambda l, j: (0, 0)),
            pl.BlockSpec((None, D_MODEL, tn), lambda l, j: (l, 0, j)),
            pl.BlockSpec((None, 1, tn), lambda l, j: (l, 0, j)),
        ],
        out_specs=pl.BlockSpec((None, nb, tn), lambda l, j: (l, 0, j)),
        out_shape=jax.ShapeDtypeStruct((depth, nb, n), F32),
        compiler_params=_cparams("parallel", "parallel"),
        name="ada_mod",
    )(c_all, ada_w, ada_b.reshape(depth, 1, n))


FFN_CHUNK = 512


def _ffn_kernel(has_final, x_ref, g_ref, sh_ref, sc_ref, gt_ref, w13_ref, w2_ref, *rest):
    if has_final:
        fg_ref, o_ref = rest
    else:
        (o_ref,) = rest
    x = x_ref[...]
    h = _norm_mod(x, g_ref[...], sh_ref[...], sc_ref[...]).astype(BF16)
    acc = None
    for c0 in range(0, D_FF, FFN_CHUNK):
        c1 = min(c0 + FFN_CHUNK, D_FF)
        g = _dot(h, w13_ref[:, c0:c1])
        up = _dot(h, w13_ref[:, D_FF + c0:D_FF + c1])
        a = (g * jax.nn.sigmoid(g) * up).astype(BF16)
        part = _dot(a, w2_ref[c0:c1, :])
        acc = part if acc is None else acc + part
    y = x + (0.5 * gt_ref[...]) * acc
    if has_final:
        y = y * lax.rsqrt(jnp.mean(y * y, axis=-1, keepdims=True) + NORM_EPS) * fg_ref[...]
    o_ref[...] = y


def ffn(x, mod4, j, g_row, w13_all, w2_all, layer, which, tm, final_g=None):
    n = x.shape[0]
    tpb = (n // mod4.shape[0]) // tm
    r = mod4.shape[2]
    mspec = lambda idx: pl.BlockSpec((None, None, r, D_MODEL), lambda i: (i // tpb, idx, 0, 0))
    in_specs = [
        pl.BlockSpec((tm, D_MODEL), lambda i: (i, 0)),
        pl.BlockSpec((1, D_MODEL), lambda i: (0, 0)),
        mspec(3 * j), mspec(3 * j + 1), mspec(3 * j + 2),
        pl.BlockSpec((None, None, D_MODEL, 2 * D_FF), lambda i: (layer, which, 0, 0)),
        pl.BlockSpec((None, None, D_FF, D_MODEL), lambda i: (layer, which, 0, 0)),
    ]
    args = [x, g_row.reshape(1, D_MODEL), mod4, mod4, mod4, w13_all, w2_all]
    if final_g is not None:
        in_specs.append(pl.BlockSpec((1, D_MODEL), lambda i: (0, 0)))
        args.append(final_g.reshape(1, D_MODEL))
    return pl.pallas_call(
        functools.partial(_ffn_kernel, final_g is not None),
        grid=(n // tm,),
        in_specs=in_specs,
        out_specs=pl.BlockSpec((tm, D_MODEL), lambda i: (i, 0)),
        out_shape=jax.ShapeDtypeStruct((n, D_MODEL), F32),
        compiler_params=_cparams("parallel"),
        name="ffn",
    )(*args)


def _nmm_kernel(mode, nw, has_bias, x_ref, g_ref, sh_ref, sc_ref, *rest):
    w_refs = rest[:nw]
    rest = rest[nw:]
    b_refs = rest[:nw] if has_bias else ()
    rest = rest[len(b_refs):]
    nout = 2 if mode == "gate3" else 1
    o_refs = rest[:nout]
    h_scr = rest[nout]

    @pl.when(pl.program_id(1) == 0)
    def _():
        h_scr[...] = _norm_mod(x_ref[...], g_ref[...], sh_ref[...], sc_ref[...]).astype(BF16)

    h = h_scr[...]
    ps = [_dot(h, w[...]) for w in w_refs]
    if has_bias:
        ps = [p + b[...] for p, b in zip(ps, b_refs)]
    if mode == "plain":
        o_refs[0][...] = ps[0]
    elif mode == "glu":
        o_refs[0][...] = ps[0] * jax.nn.sigmoid(ps[1])
    else:
        o_refs[0][...] = ps[0]
        o_refs[1][...] = ps[1] * ps[2]


def nmm(x, mod4, j, g_row, w, col_offsets, ncol, tn, tm, bias=None, mode="plain"):
    n = x.shape[0]
    nw = len(col_offsets)
    tpb = (n // mod4.shape[0]) // tm
    r = mod4.shape[2]
    mspec = lambda idx: pl.BlockSpec((None, None, r, D_MODEL), lambda i, c: (i // tpb, idx, 0, 0))
    in_specs = [
        pl.BlockSpec((tm, D_MODEL), lambda i, c: (i, 0)),
        pl.BlockSpec((1, D_MODEL), lambda i, c: (0, 0)),
        mspec(3 * j), mspec(3 * j + 1),
    ]
    args = [x, g_row.reshape(1, D_MODEL), mod4, mod4]
    for off in col_offsets:
        in_specs.append(pl.BlockSpec((D_MODEL, tn), lambda i, c, off=off: (0, off + c)))
        args.append(w)
    if bias is not None:
        b2 = bias.reshape(1, -1)
        for off in col_offsets:
            in_specs.append(pl.BlockSpec((1, tn), lambda i, c, off=off: (0, off + c)))
            args.append(b2)
    nout = 2 if mode == "gate3" else 1
    out_specs = [pl.BlockSpec((tm, tn), lambda i, c: (i, c)) for _ in range(nout)]
    out_shape = [jax.ShapeDtypeStruct((n, ncol * tn), F32) for _ in range(nout)]
    outs = pl.pallas_call(
        functools.partial(_nmm_kernel, mode, nw, bias is not None),
        grid=(n // tm, ncol),
        in_specs=in_specs,
        out_specs=out_specs,
        out_shape=out_shape,
        scratch_shapes=[pltpu.VMEM((tm, D_MODEL), BF16)],
        compiler_params=_cparams("parallel", "arbitrary"),
        name="nmm_" + mode,
    )(*args)
    return outs if nout == 2 else outs[0]


def _mmr_kernel(merge, has_bias, *refs):
    if merge:
        o1, o2, o3, l1, l2, l3 = refs[:6]
        refs = refs[6:]
        a1, a2, a3 = l1[...], l2[...], l3[...]
        m = jnp.maximum(jnp.maximum(a1, a2), a3)
        e1, e2, e3 = jnp.exp(a1 - m), jnp.exp(a2 - m), jnp.exp(a3 - m)
        inv = 1.0 / (e1 + e2 + e3)
        a = ((e1 * inv) * o1[...] + (e2 * inv) * o2[...] + (e3 * inv) * o3[...]).astype(BF16)
    else:
        a = refs[0][...].astype(BF16)
        refs = refs[1:]
    if has_bias:
        x_ref, gt_ref, w_ref, b_ref, o_ref = refs
    else:
        x_ref, gt_ref, w_ref, o_ref = refs
    y = _dot(a, w_ref[...])
    if has_bias:
        y = y + b_ref[...]
    o_ref[...] = x_ref[...] + gt_ref[...] * y


def mmr(a_list, x, mod4, j, w, tm, bias=None):
    n = x.shape[0]
    merge = len(a_list) == 6
    tpb = (n // mod4.shape[0]) // tm
    r = mod4.shape[2]
    tile = pl.BlockSpec((tm, D_MODEL), lambda i: (i, 0))
    in_specs = [tile for _ in a_list] + [
        tile,
        pl.BlockSpec((None, None, r, D_MODEL), lambda i: (i // tpb, 3 * j + 2, 0, 0)),
        pl.BlockSpec((D_MODEL, D_MODEL), lambda i: (0, 0)),
    ]
    args = list(a_list) + [x, mod4, w]
    if bias is not None:
        in_specs.append(pl.BlockSpec((1, D_MODEL), lambda i: (0, 0)))
        args.append(bias.reshape(1, D_MODEL))
    return pl.pallas_call(
        functools.partial(_mmr_kernel, merge, bias is not None),
        grid=(n // tm,),
        in_specs=in_specs,
        out_specs=tile,
        out_shape=jax.ShapeDtypeStruct((n, D_MODEL), F32),
        compiler_params=_cparams("parallel"),
        name="mmr_merge" if merge else "mmr",
    )(*args)


SUBLANES = 8


def _conv_shift_slots(ktaps):
    base = HALO - (ktaps - 1)
    shifts = sorted({(base + k) % SUBLANES for k in range(ktaps)} - {0})
    return {b: slot for slot, b in enumerate(shifts)}


def _conv_kernel(mode, ktaps, tm, rc, zero_first, halo_ref, u_ref, w_ref, *rest):
    if mode == "A":
        bdw_ref, lng_ref, lnb_ref, o_ref, cat, shifted = rest
    else:
        bg_ref, o_ref, cat, shifted = rest
    if zero_first:
        first = pl.program_id(1) == 0

        @pl.when(first)
        def _():
            cat[0:HALO, :] = jnp.zeros((HALO, D_MODEL), F32)

        @pl.when(jnp.logical_not(first))
        def _():
            cat[0:HALO, :] = halo_ref[...]
    else:
        cat[0:HALO, :] = halo_ref[...]
    cat[HALO:HALO + tm, :] = u_ref[...]
    base = HALO - (ktaps - 1)
    slots = _conv_shift_slots(ktaps)
    span = tm + HALO - SUBLANES
    for b, slot in slots.items():
        shifted[slot] = cat[b:b + span, :]
    for r0 in range(0, tm, rc):
        acc = None
        for k in range(ktaps):
            b = (base + k) % SUBLANES
            row = r0 + (base + k) - b
            src = cat[row:row + rc, :] if b == 0 else shifted[slots[b], row:row + rc, :]
            term = src * w_ref[k:k + 1, :]
            acc = term if acc is None else acc + term
        if mode == "A":
            y = acc + bdw_ref[...]
            mu = jnp.mean(y, axis=-1, keepdims=True)
            yc = y - mu
            var = jnp.mean(yc * yc, axis=-1, keepdims=True)
            yn = yc * lax.rsqrt(var + NORM_EPS) * lng_ref[...] + lnb_ref[...]
            o_ref[r0:r0 + rc, :] = (yn * jax.nn.sigmoid(yn)).astype(BF16)
        else:
            o_ref[r0:r0 + rc, :] = (bg_ref[r0:r0 + rc, :] * acc).astype(BF16)


def conv_mix(mode, u3, halo3, taps, tm, zero_first, extras):
    nb, t, _ = u3.shape
    ktaps = taps.shape[0]
    kpad = -(-ktaps // 8) * 8
    taps = jnp.pad(taps, ((0, kpad - ktaps), (0, 0)))
    rc = min(32, tm)
    hb = tm // HALO
    tile = pl.BlockSpec((None, tm, D_MODEL), lambda b, i: (b, i, 0))
    row = pl.BlockSpec((1, D_MODEL), lambda b, i: (0, 0))
    if zero_first:
        halo_spec = pl.BlockSpec((None, HALO, D_MODEL), lambda b, i: (b, jnp.maximum(i * hb - 1, 0), 0))
    else:
        halo_spec = pl.BlockSpec((None, HALO, D_MODEL), lambda b, i: (b, 0, 0))
    in_specs = [halo_spec, tile, pl.BlockSpec((kpad, D_MODEL), lambda b, i: (0, 0))]
    args = [halo3, u3, taps]
    if mode == "A":
        in_specs += [row, row, row]
        args += [e.reshape(1, D_MODEL) for e in extras]
    else:
        in_specs += [tile]
        args += list(extras)
    return pl.pallas_call(
        functools.partial(_conv_kernel, mode, ktaps, tm, rc, zero_first),
        grid=(nb, t // tm),
        in_specs=in_specs,
        out_specs=tile,
        out_shape=jax.ShapeDtypeStruct((nb, t, D_MODEL), BF16),
        scratch_shapes=[pltpu.VMEM((HALO + tm, D_MODEL), F32),
                        pltpu.VMEM((len(_conv_shift_slots(ktaps)), tm + HALO - SUBLANES, D_MODEL), F32)],
        compiler_params=_cparams("parallel", "arbitrary"),
        name="conv_" + mode,
    )(*args)


def _dil_attn_kernel(dil, q_ref, k_ref, v_ref, bias_ref, o_ref, lse_ref):
    bq = bias_ref.shape[1]
    n_sub = q_ref.shape[0] // (bq * dil)
    j = pl.program_id(1)
    lane = lax.broadcasted_iota(jnp.int32, (1, LANES), 1)
    lo = lane < HEAD_DIM
    own = (lo, jnp.logical_not(lo))
    for r in range(dil):
        k_prev = v_prev = None
        for sb in range(n_sub):
            rows = pl.ds(r + dil * bq * sb, bq, stride=dil) if dil > 1 else pl.ds(bq * sb, bq)
            q = q_ref[rows, :] * (ATTN_SCALE * LOG2E)
            k_cur = k_ref[rows, :].astype(BF16)
            v_rows = v_ref[rows, :]
            v_cur = [jnp.where(own[e], v_rows, 1.0).astype(BF16) for e in range(2)]
            if sb == 0:
                k, v, cols = k_cur, v_cur, slice(bq, 2 * bq)
            else:
                k = jnp.concatenate([k_prev, k_cur], axis=0)
                v = [jnp.concatenate([v_prev[e], v_cur[e]], axis=0) for e in range(2)]
                cols = slice(0, 2 * bq)
            k_prev, v_prev = k_cur, v_cur
            pvs, ms = [], []
            for e in range(2):
                qe = jnp.where(own[e], q, 0.0).astype(BF16)
                s = _dot_nt(qe, k) + bias_ref[2 * j + e, :, cols]
                m = jnp.max(s, axis=-1, keepdims=True)
                p = jnp.exp2(s - m)
                pvs.append(_dot(p.astype(BF16), v[e]))
                ms.append(m)
            num = jnp.where(lo, pvs[0], pvs[1])
            den = jnp.where(lo, pvs[1], pvs[0])
            den = jnp.concatenate([den[:, HEAD_DIM:], den[:, :HEAD_DIM]], axis=1)
            o_ref[rows, :] = num / den
            lse_ref[rows, :] = jnp.where(lo, ms[0], ms[1]) * (1.0 / LOG2E) + jnp.log(den)


def dilated_prompt_attn(qkv3, bias, g, dil):
    bn, s_len, _ = qkv3.shape
    nlb = D_MODEL // LANES
    ng = len(B_PAIRS)
    spec = lambda which: pl.BlockSpec((None, s_len, LANES), lambda b, j: (b, 0, (which * ng + g) * nlb + j))
    out_spec = pl.BlockSpec((None, s_len, LANES), lambda b, j: (b, 0, j))
    o, lse = pl.pallas_call(
        functools.partial(_dil_attn_kernel, dil),
        grid=(bn, nlb),
        in_specs=[spec(0), spec(1), spec(2), pl.BlockSpec(bias.shape, lambda b, j: (0, 0, 0))],
        out_specs=[out_spec, out_spec],
        out_shape=[jax.ShapeDtypeStruct((bn, s_len, D_MODEL), F32)] * 2,
        compiler_params=_cparams("parallel", "parallel"),
        name="dilated_prompt",
    )(qkv3, qkv3, qkv3, bias)
    return o.reshape(bn * s_len, D_MODEL), lse.reshape(bn * s_len, D_MODEL)


def _head_mask(ncols):
    row = lax.broadcasted_iota(jnp.int32, (N_HEADS * DEC_T, ncols), 0)
    lane = lax.broadcasted_iota(jnp.int32, (N_HEADS * DEC_T, ncols), 1)
    return (row >> 3) == (lane >> 6)


def _build_qbd(q8):
    q = q8 * ATTN_SCALE
    qt = jnp.concatenate([q] * N_HEADS, axis=0)
    return jnp.where(_head_mask(D_MODEL), qt, 0.0).astype(BF16)


def _fold_heads(x):
    xm = jnp.where(_head_mask(D_MODEL), x, 0.0)
    out = xm[0:DEC_T, :]
    for h in range(1, N_HEADS):
        out = out + xm[h * DEC_T:(h + 1) * DEC_T, :]
    return out


def _decode_init(q_ref, new_ref, bnew_ref, qbd_scr, m_scr, l_scr, acc_scr):
    qbd = _build_qbd(q_ref[...])
    qbd_scr[...] = qbd
    kn = new_ref[:, 0:D_MODEL].astype(BF16)
    vn = new_ref[:, D_MODEL:2 * D_MODEL].astype(BF16)
    s = _dot_nt(qbd, kn) + bnew_ref[...]
    m = jnp.max(s, axis=-1, keepdims=True)
    p = jnp.exp(s - m)
    m_scr[...] = m
    l_scr[...] = jnp.sum(p, axis=-1, keepdims=True)
    acc_scr[...] = _dot(p.astype(BF16), vn)


def _decode_update(kt, vt, bias, qbd_scr, m_scr, l_scr, acc_scr):
    s = _dot(qbd_scr[...], kt) + bias
    m_old = m_scr[...]
    m_new = jnp.maximum(m_old, jnp.max(s, axis=-1, keepdims=True))
    alpha = jnp.exp(m_old - m_new)
    p = jnp.exp(s - m_new)
    l_scr[...] = alpha * l_scr[...] + jnp.sum(p, axis=-1, keepdims=True)
    acc_scr[...] = alpha * acc_scr[...] + _dot_nt(p.astype(BF16), vt)
    m_scr[...] = m_new


def _bdec_kernel(q_ref, new_ref, bnew_ref, buf_ref, nxt_ref, newt_ref, bias_ref, o_ref, lse_ref, st_ref,
                 qbd_scr, m_scr, l_scr, acc_scr):
    c = pl.program_id(1)
    last = c == pl.num_programs(1) - 1

    @pl.when(c == 0)
    def _():
        _decode_init(q_ref, new_ref, bnew_ref, qbd_scr, m_scr, l_scr, acc_scr)

    _decode_update(buf_ref[0:D_MODEL, :].astype(BF16), buf_ref[D_MODEL:2 * D_MODEL, :].astype(BF16),
                   bias_ref[...], qbd_scr, m_scr, l_scr, acc_scr)

    width = buf_ref.shape[1]
    rows_blk = 256
    for r0 in range(0, 2 * D_MODEL, rows_blk):
        rs = slice(r0, r0 + rows_blk)
        tail = jnp.where(last, newt_ref[rs, :], nxt_ref[rs, :])
        x = jnp.concatenate([buf_ref[rs, :], tail], axis=1)
        st_ref[rs, :] = x[:, DEC_T:DEC_T + width]

    @pl.when(last)
    def _():
        l = l_scr[...]
        o_ref[...] = _fold_heads(acc_scr[...] / l)
        lse_ref[...] = _fold_heads(jnp.broadcast_to(m_scr[...] + jnp.log(l), (N_HEADS * DEC_T, D_MODEL)))


def dilated_decode(q3, new3, new_t, bias_new, buf_t, bias_buf):
    bn, _, L = buf_t.shape
    rows = N_HEADS * DEC_T
    rchunk = min(L, 512)
    lane_blocks = rchunk // LANES
    per_b = lambda shape: pl.BlockSpec((None,) + shape, lambda b, c: (b, 0, 0))
    o, lse, st = pl.pallas_call(
        _bdec_kernel,
        grid=(bn, L // rchunk),
        in_specs=[
            per_b((DEC_T, D_MODEL)),
            per_b((NEW_PAD, 2 * D_MODEL)),
            pl.BlockSpec((rows, NEW_PAD), lambda b, c: (0, 0)),
            pl.BlockSpec((None, 2 * D_MODEL, rchunk), lambda b, c: (b, 0, c)),
            pl.BlockSpec((None, 2 * D_MODEL, LANES),
                         lambda b, c: (b, 0, jnp.minimum((c + 1) * lane_blocks, L // LANES - 1))),
            per_b((2 * D_MODEL, LANES)),
            pl.BlockSpec((rows, rchunk), lambda b, c: (0, c)),
        ],
        out_specs=[per_b((DEC_T, D_MODEL)), per_b((DEC_T, D_MODEL)),
                   pl.BlockSpec((None, 2 * D_MODEL, rchunk), lambda b, c: (b, 0, c))],
        out_shape=[jax.ShapeDtypeStruct((bn, DEC_T, D_MODEL), F32)] * 2
                  + [jax.ShapeDtypeStruct((bn, 2 * D_MODEL, L), F32)],
        scratch_shapes=[pltpu.VMEM((rows, D_MODEL), BF16), pltpu.VMEM((rows, 1), F32),
                        pltpu.VMEM((rows, 1), F32), pltpu.VMEM((rows, D_MODEL), F32)],
        compiler_params=_cparams("parallel", "arbitrary"),
        name="dilated_decode",
    )(q3, new3, bias_new, buf_t, buf_t, new_t, bias_buf)
    return o.reshape(bn * DEC_T, D_MODEL), lse.reshape(bn * DEC_T, D_MODEL), st


def _fgate_kernel(seg_shift, use_carry, p_ref, b_ref, logf_ref, cum_ref, carry):
    tt = p_ref.shape[0]
    x = p_ref[...] + b_ref[...]
    lf = jnp.minimum(x, 0.0) - jnp.log(1.0 + jnp.exp(-jnp.abs(x)))
    logf_ref[...] = lf
    r = lax.broadcasted_iota(jnp.int32, (tt, tt), 0)
    c = lax.broadcasted_iota(jnp.int32, (tt, tt), 1)
    tri = jnp.where(c <= r, 1.0, 0.0)
    if seg_shift is not None:
        tri = jnp.where((r >> seg_shift) == (c >> seg_shift), tri, 0.0)
    cum = _dot_exact(tri, lf)
    if use_carry:
        @pl.when(pl.program_id(1) == 0)
        def _():
            carry[...] = jnp.zeros_like(carry)

        cum = cum + carry[...]
        carry[...] = cum[tt - 1:tt, :]
    cum_ref[...] = cum


def fgate(p3, b_f, tt, seg_shift, use_carry):
    bn, t, ncols = p3.shape
    fcol = (ncols - LANES) // LANES
    b_pad = jnp.pad(b_f.reshape(1, -1), ((0, 0), (0, LANES - b_f.shape[-1])))
    out_spec = pl.BlockSpec((None, tt, LANES), lambda b, i: (b, i, 0))
    return pl.pallas_call(
        functools.partial(_fgate_kernel, seg_shift, use_carry),
        grid=(bn, t // tt),
        in_specs=[pl.BlockSpec((None, tt, LANES), lambda b, i: (b, i, fcol)),
                  pl.BlockSpec((1, LANES), lambda b, i: (0, 0))],
        out_specs=[out_spec, out_spec],
        out_shape=[jax.ShapeDtypeStruct((bn, t, LANES), F32)] * 2,
        scratch_shapes=[pltpu.VMEM((1, LANES), F32)],
        compiler_params=_cparams("parallel", "arbitrary"),
        name="fgate",
    )(p3, b_pad)


def _split3(x):
    hi = x.astype(BF16)
    r1 = x - hi.astype(F32)
    mid = r1.astype(BF16)
    lo = (r1 - mid.astype(F32)).astype(BF16)
    return jnp.concatenate([hi, mid, lo], axis=1)


def _piece_selector(h, first_lane, sign):
    rho = lax.broadcasted_iota(jnp.int32, (3 * LANES, LANES), 0)
    lam = lax.broadcasted_iota(jnp.int32, (3 * LANES, LANES), 1)
    at_target = jnp.where(lam == first_lane + (rho >> 7), sign, 0.0)
    return jnp.where((rho & (LANES - 1)) == h, at_target, 0.0).astype(BF16)


def _fox_flash_kernel(tq, q_ref, k_ref, v_ref, cq_ref, ck_ref, o_ref, kvt_ref, kaug, vaug, m_scr, acc_scr):
    hp = pl.program_id(1)
    qi = pl.program_id(2)
    lane = lax.broadcasted_iota(jnp.int32, (1, LANES), 1)
    lo = lane < HEAD_DIM
    own = (lo, jnp.logical_not(lo))
    base = (HEAD_DIM, 0)

    def ones_at(first):
        return jnp.where(lane >= first, jnp.where(lane < first + 3, 1.0, 0.0), 0.0)

    @pl.when(qi == 0)
    def _():
        pieces = _split3(ck_ref[...] * LOG2E)
        for e in range(2):
            extras = _dot(pieces, _piece_selector(2 * hp + e, base[e], -1.0)) + ones_at(base[e] + 3)
            kaug[e] = jnp.where(own[e], k_ref[...], extras).astype(BF16)
            vaug[e] = jnp.where(own[e], v_ref[...], 1.0).astype(BF16)
        for pg in range(kvt_ref.shape[0]):
            rs = slice(pg * PAGE, (pg + 1) * PAGE)
            kvt_ref[pg, 0] = k_ref[rs, :].T
            kvt_ref[pg, 1] = v_ref[rs, :].T

    qpieces = _split3(cq_ref[...] * LOG2E)
    qs = q_ref[...] * (ATTN_SCALE * LOG2E)
    qaug = []
    for e in range(2):
        extras = _dot(qpieces, _piece_selector(2 * hp + e, base[e] + 3, 1.0)) + ones_at(base[e])
        qaug.append(jnp.where(own[e], qs, extras).astype(BF16))
    m_scr[...] = jnp.full(m_scr.shape, NEG_INF, F32)
    acc_scr[...] = jnp.zeros_like(acc_scr)

    def tile(kt, diagonal):
        ks = pl.multiple_of(kt * tq, tq)
        for e in range(2):
            s = _dot_nt(qaug[e], kaug[e, pl.ds(ks, tq), :])
            if diagonal:
                r = lax.broadcasted_iota(jnp.int32, (tq, tq), 0)
                c = lax.broadcasted_iota(jnp.int32, (tq, tq), 1)
                s = jnp.where(r >= c, s, NEG_INF)
            m_old = m_scr[e]
            m_new = jnp.maximum(m_old, jnp.max(s, axis=-1, keepdims=True))
            alpha = jnp.exp2(m_old - m_new)
            p = jnp.exp2(s - jnp.concatenate([m_new] * (tq // LANES), axis=1))
            acc_scr[e] = alpha * acc_scr[e] + _dot(p.astype(BF16), vaug[e, pl.ds(ks, tq), :])
            m_scr[e] = m_new

    def body(kt, carry):
        tile(kt, False)
        return carry

    lax.fori_loop(0, qi, body, 0)
    tile(qi, True)
    a0, a1 = acc_scr[0], acc_scr[1]
    num = jnp.where(lo, a0, a1)
    den = jnp.where(lo, a1, a0)
    den = jnp.concatenate([den[:, HEAD_DIM:], den[:, :HEAD_DIM]], axis=1)
    o_ref[...] = (num / den).astype(BF16)


def fox_flash(p3, cum3):
    bn, s_len, _ = p3.shape
    tq = 512
    nh2 = N_HEADS // 2
    n_pg = s_len // PAGE
    return pl.pallas_call(
        functools.partial(_fox_flash_kernel, tq),
        grid=(bn, nh2, s_len // tq),
        in_specs=[
            pl.BlockSpec((None, tq, LANES), lambda b, h, i: (b, i, h)),
            pl.BlockSpec((None, s_len, LANES), lambda b, h, i: (b, 0, nh2 + h)),
            pl.BlockSpec((None, s_len, LANES), lambda b, h, i: (b, 0, 2 * nh2 + h)),
            pl.BlockSpec((None, tq, LANES), lambda b, h, i: (b, i, 0)),
            pl.BlockSpec((None, s_len, LANES), lambda b, h, i: (b, 0, 0)),
        ],
        out_specs=[pl.BlockSpec((None, tq, LANES), lambda b, h, i: (b, i, h)),
                   pl.BlockSpec((None, n_pg, 2, None, LANES, PAGE), lambda b, h, i: (b, 0, 0, h, 0, 0))],
        out_shape=[jax.ShapeDtypeStruct((bn, s_len, D_MODEL), BF16),
                   jax.ShapeDtypeStruct((bn, n_pg, 2, nh2, LANES, PAGE), F32)],
        scratch_shapes=[pltpu.VMEM((2, s_len, LANES), BF16), pltpu.VMEM((2, s_len, LANES), BF16),
                        pltpu.VMEM((2, tq, LANES), F32), pltpu.VMEM((2, tq, LANES), F32)],
        compiler_params=_cparams("parallel", "parallel", "arbitrary"),
        name="fox_flash",
    )(p3, p3, p3, cum3, cum3)


def _suffix_sums(x):
    lane = lax.broadcasted_iota(jnp.int32, x.shape, 1)
    y = x
    step = 1
    while step < x.shape[1]:
        ahead = pltpu.roll(y, x.shape[1] - step, axis=1)
        y = y + jnp.where(lane < x.shape[1] - step, ahead, 0.0)
        step *= 2
    return y


def _rows_per_head(x):
    return jnp.concatenate([jnp.broadcast_to(x[h:h + 1, :], (DEC_T, x.shape[1])) for h in range(N_HEADS)],
                           axis=0)


def _foxdec_kernel(npg, pt_ref, q_ref, new_ref, cncol_ref, bnew_ref, *rest):
    kv_refs = rest[:npg]
    lf_refs = rest[npg:2 * npg]
    o_ref, qbd_scr, m_scr, l_scr, acc_scr, carry, kcat, vcat = rest[2 * npg:]
    j = pl.program_id(1)

    @pl.when(j == 0)
    def _():
        _decode_init(q_ref, new_ref, bnew_ref, qbd_scr, m_scr, l_scr, acc_scr)
        carry[...] = jnp.zeros_like(carry)

    after = carry[...]
    biases = []
    for i in range(npg):
        lf = lf_refs[i][...]
        incl = _suffix_sums(lf)
        biases.append(after + (incl - lf))
        after = after + incl[:, 0:1]
        kcat[:, i * PAGE:(i + 1) * PAGE] = kv_refs[i][0:D_MODEL, :].astype(BF16)
        vcat[:, i * PAGE:(i + 1) * PAGE] = kv_refs[i][D_MODEL:2 * D_MODEL, :].astype(BF16)
    carry[...] = after
    bias = _rows_per_head(jnp.concatenate(biases, axis=1)) + cncol_ref[...]
    _decode_update(kcat[...], vcat[...], bias, qbd_scr, m_scr, l_scr, acc_scr)

    @pl.when(j == pl.num_programs(1) - 1)
    def _():
        o_ref[...] = _fold_heads(acc_scr[...] / l_scr[...])


def fox_decode(page_table, q3, new3, cn_col, bias_new, cache_kvt, cache_lft, u):
    bn, n_pages = page_table.shape
    npg = 8
    assert n_pages % npg == 0
    rows = N_HEADS * DEC_T

    def page_spec(i, nrows):
        def imap(b, j, pt):
            return (u, pt[b, n_pages - 1 - (j * npg + i)], 0, 0)
        return pl.BlockSpec((None, None, nrows, PAGE), imap)

    per_b = lambda shape: pl.BlockSpec((None,) + shape, lambda b, j, pt: (b, 0, 0))
    grid_spec = pltpu.PrefetchScalarGridSpec(
        num_scalar_prefetch=1,
        grid=(bn, n_pages // npg),
        in_specs=[per_b((DEC_T, D_MODEL)), per_b((NEW_PAD, 2 * D_MODEL)), per_b((rows, 1)),
                  per_b((rows, NEW_PAD))]
                 + [page_spec(i, 2 * D_MODEL) for i in range(npg)]
                 + [page_spec(i, N_HEADS) for i in range(npg)],
        out_specs=per_b((DEC_T, D_MODEL)),
        scratch_shapes=[pltpu.VMEM((rows, D_MODEL), BF16), pltpu.VMEM((rows, 1), F32),
                        pltpu.VMEM((rows, 1), F32), pltpu.VMEM((rows, D_MODEL), F32),
                        pltpu.VMEM((N_HEADS, PAGE), F32),
                        pltpu.VMEM((D_MODEL, npg * PAGE), BF16), pltpu.VMEM((D_MODEL, npg * PAGE), BF16)],
    )
    o = pl.pallas_call(
        functools.partial(_foxdec_kernel, npg),
        grid_spec=grid_spec,
        out_shape=jax.ShapeDtypeStruct((bn, DEC_T, D_MODEL), F32),
        compiler_params=_cparams("parallel", "arbitrary"),
        name="fox_decode",
    )(page_table, q3, new3, cn_col, bias_new, *([cache_kvt] * npg), *([cache_lft] * npg))
    return o.reshape(bn * DEC_T, D_MODEL)


def _t5_bucket(dist):
    max_exact = REL_BUCKETS // 2
    df = jnp.maximum(dist, 1).astype(F32)
    large = max_exact + (jnp.log(df / max_exact) / math.log(REL_MAX_DIST / max_exact)
                         * (REL_BUCKETS - max_exact)).astype(jnp.int32)
    large = jnp.minimum(large, REL_BUCKETS - 1)
    return jnp.where(dist < max_exact, dist, large)


def _bias_table(tab, dist, valid):
    onehot = (_t5_bucket(dist)[..., None] == jnp.arange(REL_BUCKETS)).astype(F32)
    b = jnp.einsum("qkb,bh->hqk", onehot, tab.astype(F32), precision=lax.Precision.HIGHEST)
    return jnp.where(valid[None], b, NEG_INF)


def _prompt_bias(tab, dil, ns, bq):
    steps = jnp.arange(bq)[:, None] - jnp.arange(2 * bq)[None, :] + ns
    valid = (steps >= 0) & (steps <= ns)
    return _bias_table(tab, jnp.clip(steps, 0, ns) * dil, valid)


def _decode_bias(tab, dil, ns, lb):
    t = jnp.arange(DEC_T)

    def table(dist):
        valid = (dist >= 0) & (dist % dil == 0) & (dist <= ns * dil)
        return _bias_table(tab, jnp.maximum(dist, 0), valid).reshape(N_HEADS * DEC_T, dist.shape[1])

    d_buf = lb + t[:, None] - jnp.arange(lb)[None, :]
    s = jnp.arange(NEW_PAD)
    d_new = jnp.where(s[None, :] < DEC_T, t[:, None] - s[None, :], -1)
    return table(d_buf), table(d_new)


def _bcol(which, g):
    c0 = (which * len(B_PAIRS) + g) * D_MODEL
    return slice(c0, c0 + D_MODEL)


def dilated_prompt_stage(qkv3, rel_bias):
    bn, s_len, _ = qkv3.shape
    outs, lses, states = [], [], []
    for g, (win, dil) in enumerate(B_PAIRS):
        ns = win // dil
        assert ns == 128 and s_len % (128 * dil) == 0
        tab = rel_bias[:, g * N_HEADS:(g + 1) * N_HEADS]
        o, lse = dilated_prompt_attn(qkv3, _prompt_bias(tab, dil, ns, 128) * LOG2E, g, dil)
        outs.append(o)
        lses.append(lse)
        lw = min(win, s_len)
        states.append(jnp.stack([qkv3[:, s_len - lw:, _bcol(1, g)], qkv3[:, s_len - lw:, _bcol(2, g)]],
                                axis=2).reshape(bn, lw, 2, N_HEADS, HEAD_DIM))
    return outs, lses, states


def dilated_sample_stage(qkv3, bufs, rel_bias):
    dn, t_new, _ = qkv3.shape
    outs, lses, states = [], [], []
    for g, (win, dil) in enumerate(B_PAIRS):
        ns = win // dil
        tab = rel_bias[:, g * N_HEADS:(g + 1) * N_HEADS]
        lb = bufs[g].shape[1]
        assert lb == ns * dil
        buf_t = bufs[g].transpose(0, 2, 3, 4, 1).reshape(dn, 2 * D_MODEL, lb)
        new_kv = jnp.concatenate([qkv3[:, :, _bcol(1, g)], qkv3[:, :, _bcol(2, g)]], axis=-1)
        bias_buf, bias_new = _decode_bias(tab, dil, ns, lb)
        new_t = jnp.pad(new_kv.transpose(0, 2, 1), ((0, 0), (0, 0), (0, LANES - t_new)))
        o, lse, st = dilated_decode(qkv3[:, :, _bcol(0, g)],
                                    jnp.pad(new_kv, ((0, 0), (0, NEW_PAD - t_new), (0, 0))),
                                    new_t, bias_new, buf_t, bias_buf)
        outs.append(o)
        lses.append(lse)
        states.append(st.reshape(dn, 2, N_HEADS, HEAD_DIM, lb).transpose(0, 4, 1, 2, 3))
    return outs, lses, states


def fox_prompt_stage(p3, b_f):
    bn, s_len, _ = p3.shape
    nf = b_f.shape[-1]
    logf, cum = fgate(p3, b_f, 512, None, True)
    o, kvt = fox_flash(p3, cum)
    kv_pages = kvt.reshape(bn, s_len // PAGE, 2, N_HEADS, HEAD_DIM, PAGE).transpose(0, 1, 5, 2, 3, 4)
    return o, kv_pages, logf[:, :, :nf].reshape(bn, s_len // PAGE, PAGE, nf)


def fox_sample_stage(p3, b_f, page_table, cache_c_kv, cache_c_logf, u):
    dn, t_new, _ = p3.shape
    nf = b_f.shape[-1]
    logf, cn = fgate(p3.reshape(1, dn * t_new, -1), b_f, dn * t_new, 3, False)
    cn_t = cn[0, :, :N_HEADS].reshape(dn, t_new, N_HEADS).transpose(0, 2, 1)
    cn_col = cn_t.reshape(dn, N_HEADS * t_new, 1)
    cn_keys = jnp.broadcast_to(cn_t[:, :, None, :], (dn, N_HEADS, t_new, t_new))
    cn_keys = cn_keys.reshape(dn, N_HEADS * t_new, t_new)
    tq_idx = jnp.tile(jnp.arange(t_new), N_HEADS)[:, None]
    causal = jnp.arange(t_new)[None, :] <= tq_idx
    bias_new = jnp.where(causal[None], cn_col - cn_keys, NEG_INF)
    bias_new = jnp.pad(bias_new, ((0, 0), (0, 0), (0, NEW_PAD - t_new)), constant_values=NEG_INF)
    new_kv = p3[:, :, D_MODEL:3 * D_MODEL]
    nc, n_phys = cache_c_kv.shape[:2]
    kvt = cache_c_kv.transpose(0, 1, 3, 4, 5, 2).reshape(nc, n_phys, 2 * D_MODEL, PAGE)
    lft = cache_c_logf.transpose(0, 1, 3, 2)
    o = fox_decode(page_table, p3[:, :, :D_MODEL],
                   jnp.pad(new_kv, ((0, 0), (0, NEW_PAD - t_new), (0, 0))), cn_col, bias_new, kvt, lft, u)
    return (o, new_kv.reshape(dn, t_new, 2, N_HEADS, HEAD_DIM),
            logf[0, :, :nf].reshape(dn, t_new, nf))


def kernel(x_prompt, x_sample, c_prompt, c_sample, state_a_conv, state_b_kv_w128, state_b_kv_w512, state_b_kv_w2048, cache_c_kv, cache_c_logf, page_table, state_d_conv, ada_w, ada_b, norm_g, final_g, ffn_w13, ffn_w2, a_w_in, a_b_in, a_w_dw, a_b_dw, a_ln_g, a_ln_b, a_w_out, a_b_out, b_w_qkv, b_w_out, rel_bias, c_w_in, c_b_f, c_w_out, d_w_in, d_w_conv, d_w_out):
    bn, s_len, _ = x_prompt.shape
    dn, t_new, _ = x_sample.shape
    assert t_new == DEC_T and s_len % 512 == 0
    n_p, n_s = bn * s_len, dn * t_new
    depth = ada_w.shape[0]
    tm_p, tm_s = 1024, n_s
    tm_ffn = 512
    b_states = (state_b_kv_w128, state_b_kv_w512, state_b_kv_w2048)
    w13_all, w2_all = ffn_w13.astype(BF16), ffn_w2.astype(BF16)

    mod = ada_mod(jnp.concatenate([c_prompt, c_sample], axis=0), ada_w, ada_b)
    mod = mod.reshape(depth, bn + dn, 9, D_MODEL)

    xp = x_prompt.reshape(n_p, D_MODEL)
    xs = x_sample.reshape(n_s, D_MODEL)
    outs = {}
    for l in range(depth):
        kind, u = l % 4, l // 4
        mod_p = mod[l, :bn].reshape(bn, 9, 1, D_MODEL)
        mod_s = jnp.repeat(mod[l, bn:], t_new, axis=0).transpose(1, 0, 2)[None]
        ffn_paths = lambda xp, xs: ((xp, mod_p, tm_ffn), (xs, mod_s, tm_s))

        xp, xs = [ffn(x, m, 0, norm_g[l, 0], w13_all, w2_all, l, 0, tm) for x, m, tm in ffn_paths(xp, xs)]
        paths = ((xp, mod_p, tm_p), (xs, mod_s, tm_s))
        wide_paths = ((xp, mod_p, 2 * tm_p), (xs, mod_s, tm_s))

        if kind == 0:
            w_in, w_out = a_w_in[u].astype(BF16), a_w_out[u].astype(BF16)
            tn = 512
            us = [nmm(x, m, 1, norm_g[l, 1], w_in, (0, D_MODEL // tn), D_MODEL // tn, tn, tm,
                      bias=a_b_in[u], mode="glu") for x, m, tm in paths]
            up3, us3 = us[0].reshape(bn, s_len, D_MODEL), us[1].reshape(dn, t_new, D_MODEL)
            extras = (a_b_dw[u], a_ln_g[u], a_ln_b[u])
            yp = conv_mix("A", up3, up3, a_w_dw[u], 256, True, extras)
            halo_s = jnp.pad(state_a_conv[u], ((0, 0), (HALO - (CONV_A - 1), 0), (0, 0)))
            ys = conv_mix("A", us3, halo_s, a_w_dw[u], t_new, False, extras)
            xp = mmr([yp.reshape(n_p, D_MODEL)], xp, mod_p, 1, w_out, tm_p, bias=a_b_out[u])
            xs = mmr([ys.reshape(n_s, D_MODEL)], xs, mod_s, 1, w_out, tm_s, bias=a_b_out[u])
            outs.setdefault("a_p", []).append(up3[:, s_len - (CONV_A - 1):])
            outs.setdefault("a_s", []).append(
                jnp.concatenate([state_a_conv[u], us3], axis=1)[:, t_new:])
        elif kind == 1:
            w_qkv, w_out = b_w_qkv[u].astype(BF16), b_w_out[u].astype(BF16)
            ncol = w_qkv.shape[1] // 1024
            qkv_p, qkv_s = [nmm(x, m, 1, norm_g[l, 1], w_qkv, (0,), ncol, 1024, tm)
                            for x, m, tm in wide_paths]
            po, pl_, pst = dilated_prompt_stage(qkv_p.reshape(bn, s_len, -1), rel_bias)
            so, sl_, sst = dilated_sample_stage(qkv_s.reshape(dn, t_new, -1),
                                                [b[u] for b in b_states], rel_bias)
            xp = mmr(po + pl_, xp, mod_p, 1, w_out, tm_p // 2)
            xs = mmr(so + sl_, xs, mod_s, 1, w_out, tm_s)
            for g in range(len(B_PAIRS)):
                outs.setdefault("b%d_p" % g, []).append(pst[g])
                outs.setdefault("b%d_s" % g, []).append(sst[g])
        elif kind == 2:
            nf = c_b_f.shape[-1]
            w_in = jnp.pad(c_w_in[u], ((0, 0), (0, LANES - nf))).astype(BF16)
            w_out = c_w_out[u].astype(BF16)
            tn = 640
            ncol = w_in.shape[1] // tn
            pp, ps = [nmm(x, m, 1, norm_g[l, 1], w_in, (0,), ncol, tn, tm) for x, m, tm in wide_paths]
            o_p, kv_p, lf_p = fox_prompt_stage(pp.reshape(bn, s_len, -1), c_b_f[u])
            o_s, kv_s, lf_s = fox_sample_stage(ps.reshape(dn, t_new, -1), c_b_f[u], page_table,
                                               cache_c_kv, cache_c_logf, u)
            xp = mmr([o_p.reshape(n_p, D_MODEL)], xp, mod_p, 1, w_out, tm_p)
            xs = mmr([o_s], xs, mod_s, 1, w_out, tm_s)
            outs.setdefault("ckv_p", []).append(kv_p)
            outs.setdefault("clf_p", []).append(lf_p)
            outs.setdefault("ckv_s", []).append(kv_s)
            outs.setdefault("clf_s", []).append(lf_s)
        else:
            w_in, w_out = d_w_in[u].astype(BF16), d_w_out[u].astype(BF16)
            tn = 512
            nc = D_MODEL // tn
            (bg_p, z_p), (bg_s, z_s) = [
                nmm(x, m, 1, norm_g[l, 1], w_in, (0, nc, 2 * nc), nc, tn, tm, mode="gate3")
                for x, m, tm in paths]
            zp3, zs3 = z_p.reshape(bn, s_len, D_MODEL), z_s.reshape(dn, t_new, D_MODEL)
            yp = conv_mix("D", zp3, zp3, d_w_conv[u], 256, True, (bg_p.reshape(bn, s_len, D_MODEL),))
            halo_s = jnp.pad(state_d_conv[u], ((0, 0), (HALO - (CONV_D - 1), 0), (0, 0)))
            ys = conv_mix("D", zs3, halo_s, d_w_conv[u], t_new, False, (bg_s.reshape(dn, t_new, D_MODEL),))
            xp = mmr([yp.reshape(n_p, D_MODEL)], xp, mod_p, 1, w_out, tm_p)
            xs = mmr([ys.reshape(n_s, D_MODEL)], xs, mod_s, 1, w_out, tm_s)
            outs.setdefault("d_p", []).append(zp3[:, s_len - (CONV_D - 1):])
            outs.setdefault("d_s", []).append(
                jnp.concatenate([state_d_conv[u], zs3], axis=1)[:, t_new:])

        fg = final_g if l == depth - 1 else None
        xp, xs = [ffn(x, m, 2, norm_g[l, 2], w13_all, w2_all, l, 1, tm, final_g=fg)
                  for x, m, tm in ffn_paths(xp, xs)]

    st = lambda key: jnp.stack(outs[key])
    return (xp.reshape(bn, s_len, D_MODEL), xs.reshape(dn, t_new, D_MODEL),
            st("a_p"), st("a_s"), st("b0_p"), st("b0_s"), st("b1_p"), st("b1_s"), st("b2_p"), st("b2_s"),
            st("ckv_p"), st("clf_p"), st("ckv_s"), st("clf_s"), st("d_p"), st("d_s"))
```

```python
import functools
import math

import jax
import jax.numpy as jnp
from jax import lax
from jax.experimental import pallas as pl
from jax.experimental.pallas import tpu as pltpu

F32 = jnp.float32
BF16 = jnp.bfloat16

D_MODEL = 1024
D_FF = 2816
NORM_EPS = 1e-6
HEAD_DIM = 64
N_HEADS = 16
ATTN_SCALE = HEAD_DIM ** -0.5
LOG2E = math.log2(math.e)
NEG_INF = -1e30
CONV_A = 31
CONV_D = 3
B_PAIRS = ((128, 1), (512, 4), (2048, 16))
REL_BUCKETS = 32
REL_MAX_DIST = 2048
PAGE = 128
DEC_T = 8

LANES = 128
HALO = 32
NEW_PAD = 16
VMEM_LIMIT = 56 * 1024 * 1024

NT_DIMS = (((1,), (1,)), ((), ()))


def _cparams(*sem):
    return pltpu.CompilerParams(dimension_semantics=sem, vmem_limit_bytes=VMEM_LIMIT)


def _dot(a, b):
    return jnp.dot(a, b, preferred_element_type=F32)


def _dot_nt(a, b):
    return lax.dot_general(a, b, NT_DIMS, preferred_element_type=F32)


def _dot_exact(a, b):
    return jnp.dot(a, b, preferred_element_type=F32, precision=lax.Precision.HIGHEST)


def _norm_mod(x, g, shift, scale):
    y = x * lax.rsqrt(jnp.mean(x * x, axis=-1, keepdims=True) + NORM_EPS) * g
    return y * (1.0 + scale) + shift


def _ada_kernel(c_ref, w_ref, b_ref, o_ref):
    c = c_ref[...]
    h = (c * jax.nn.sigmoid(c)).astype(BF16)
    o_ref[...] = _dot(h, w_ref[...].astype(BF16)) + b_ref[...]


def ada_mod(c_all, ada_w, ada_b):
    depth, _, n = ada_w.shape
    nb = c_all.shape[0]
    tn = 1024
    return pl.pallas_call(
        _ada_kernel,
        grid=(depth, n // tn),
        in_specs=[
            pl.BlockSpec((nb, D_MODEL), lambda l, j: (0, 0)),
            pl.BlockSpec((None, D_MODEL, tn), lambda l, j: (l, 0, j)),
            pl.BlockSpec((None, 1, tn), lambda l, j: (l, 0, j)),
        ],
        out_specs=pl.BlockSpec((None, nb, tn), lambda l, j: (l, 0, j)),
        out_shape=jax.ShapeDtypeStruct((depth, nb, n), F32),
        compiler_params=_cparams("parallel", "parallel"),
        name="ada_mod",
    )(c_all, ada_w, ada_b.reshape(depth, 1, n))


FFN_CHUNK = 512


def _ffn_kernel(has_final, x_ref, g_ref, sh_ref, sc_ref, gt_ref, w13_ref, w2_ref, *rest):
    if has_final:
        fg_ref, o_ref = rest
    else:
        (o_ref,) = rest
    x = x_ref[...]
    h = _norm_mod(x, g_ref[...], sh_ref[...], sc_ref[...]).astype(BF16)
    acc = None
    for c0 in range(0, D_FF, FFN_CHUNK):
        c1 = min(c0 + FFN_CHUNK, D_FF)
        g = _dot(h, w13_ref[:, c0:c1])
        up = _dot(h, w13_ref[:, D_FF + c0:D_FF + c1])
        a = (g * jax.nn.sigmoid(g) * up).astype(BF16)
        part = _dot(a, w2_ref[c0:c1, :])
        acc = part if acc is None else acc + part
    y = x + (0.5 * gt_ref[...]) * acc
    if has_final:
        y = y * lax.rsqrt(jnp.mean(y * y, axis=-1, keepdims=True) + NORM_EPS) * fg_ref[...]
    o_ref[...] = y


def ffn(x, mod4, j, g_row, w13_all, w2_all, layer, which, tm, final_g=None):
    n = x.shape[0]
    tpb = (n // mod4.shape[0]) // tm
    r = mod4.shape[2]
    mspec = lambda idx: pl.BlockSpec((None, None, r, D_MODEL), lambda i: (i // tpb, idx, 0, 0))
    in_specs = [
        pl.BlockSpec((tm, D_MODEL), lambda i: (i, 0)),
        pl.BlockSpec((1, D_MODEL), lambda i: (0, 0)),
        mspec(3 * j), mspec(3 * j + 1), mspec(3 * j + 2),
        pl.BlockSpec((None, None, D_MODEL, 2 * D_FF), lambda i: (layer, which, 0, 0)),
        pl.BlockSpec((None, None, D_FF, D_MODEL), lambda i: (layer, which, 0, 0)),
    ]
    args = [x, g_row.reshape(1, D_MODEL), mod4, mod4, mod4, w13_all, w2_all]
    if final_g is not None:
        in_specs.append(pl.BlockSpec((1, D_MODEL), lambda i: (0, 0)))
        args.append(final_g.reshape(1, D_MODEL))
    return pl.pallas_call(
        functools.partial(_ffn_kernel, final_g is not None),
        grid=(n // tm,),
        in_specs=in_specs,
        out_specs=pl.BlockSpec((tm, D_MODEL), lambda i: (i, 0)),
        out_shape=jax.ShapeDtypeStruct((n, D_MODEL), F32),
        compiler_params=_cparams("parallel"),
        name="ffn",
    )(*args)


def _nmm_kernel(mode, nw, has_bias, x_ref, g_ref, sh_ref, sc_ref, *rest):
    w_refs = rest[:nw]
    rest = rest[nw:]
    b_refs = rest[:nw] if has_bias else ()
    rest = rest[len(b_refs):]
    nout = 2 if mode == "gate3" else 1
    o_refs = rest[:nout]
    h_scr = rest[nout]

    @pl.when(pl.program_id(1) == 0)
    def _():
        h_scr[...] = _norm_mod(x_ref[...], g_ref[...], sh_ref[...], sc_ref[...]).astype(BF16)

    h = h_scr[...]
    ps = [_dot(h, w[...]) for w in w_refs]
    if has_bias:
        ps = [p + b[...] for p, b in zip(ps, b_refs)]
    if mode == "plain":
        o_refs[0][...] = ps[0]
    elif mode == "glu":
        o_refs[0][...] = ps[0] * jax.nn.sigmoid(ps[1])
    else:
        o_refs[0][...] = ps[0]
        o_refs[1][...] = ps[1] * ps[2]


def nmm(x, mod4, j, g_row, w, col_offsets, ncol, tn, tm, bias=None, mode="plain"):
    n = x.shape[0]
    nw = len(col_offsets)
    tpb = (n // mod4.shape[0]) // tm
    r = mod4.shape[2]
    mspec = lambda idx: pl.BlockSpec((None, None, r, D_MODEL), lambda i, c: (i // tpb, idx, 0, 0))
    in_specs = [
        pl.BlockSpec((tm, D_MODEL), lambda i, c: (i, 0)),
        pl.BlockSpec((1, D_MODEL), lambda i, c: (0, 0)),
        mspec(3 * j), mspec(3 * j + 1),
    ]
    args = [x, g_row.reshape(1, D_MODEL), mod4, mod4]
    for off in col_offsets:
        in_specs.append(pl.BlockSpec((D_MODEL, tn), lambda i, c, off=off: (0, off + c)))
        args.append(w)
    if bias is not None:
        b2 = bias.reshape(1, -1)
        for off in col_offsets:
            in_specs.append(pl.BlockSpec((1, tn), lambda i, c, off=off: (0, off + c)))
            args.append(b2)
    nout = 2 if mode == "gate3" else 1
    out_specs = [pl.BlockSpec((tm, tn), lambda i, c: (i, c)) for _ in range(nout)]
    out_shape = [jax.ShapeDtypeStruct((n, ncol * tn), F32) for _ in range(nout)]
    outs = pl.pallas_call(
        functools.partial(_nmm_kernel, mode, nw, bias is not None),
        grid=(n // tm, ncol),
        in_specs=in_specs,
        out_specs=out_specs,
        out_shape=out_shape,
        scratch_shapes=[pltpu.VMEM((tm, D_MODEL), BF16)],
        compiler_params=_cparams("parallel", "arbitrary"),
        name="nmm_" + mode,
    )(*args)
    return outs if nout == 2 else outs[0]


def _mmr_kernel(merge, has_bias, *refs):
    if merge:
        o1, o2, o3, l1, l2, l3 = refs[:6]
        refs = refs[6:]
        a1, a2, a3 = l1[...], l2[...], l3[...]
        m = jnp.maximum(jnp.maximum(a1, a2), a3)
        e1, e2, e3 = jnp.exp(a1 - m), jnp.exp(a2 - m), jnp.exp(a3 - m)
        inv = 1.0 / (e1 + e2 + e3)
        a = ((e1 * inv) * o1[...] + (e2 * inv) * o2[...] + (e3 * inv) * o3[...]).astype(BF16)
    else:
        a = refs[0][...].astype(BF16)
        refs = refs[1:]
    if has_bias:
        x_ref, gt_ref, w_ref, b_ref, o_ref = refs
    else:
        x_ref, gt_ref, w_ref, o_ref = refs
    y = _dot(a, w_ref[...])
    if has_bias:
        y = y + b_ref[...]
    o_ref[...] = x_ref[...] + gt_ref[...] * y


def mmr(a_list, x, mod4, j, w, tm, bias=None):
    n = x.shape[0]
    merge = len(a_list) == 6
    tpb = (n // mod4.shape[0]) // tm
    r = mod4.shape[2]
    tile = pl.BlockSpec((tm, D_MODEL), lambda i: (i, 0))
    in_specs = [tile for _ in a_list] + [
        tile,
        pl.BlockSpec((None, None, r, D_MODEL), lambda i: (i // tpb, 3 * j + 2, 0, 0)),
        pl.BlockSpec((D_MODEL, D_MODEL), lambda i: (0, 0)),
    ]
    args = list(a_list) + [x, mod4, w]
    if bias is not None:
        in_specs.append(pl.BlockSpec((1, D_MODEL), lambda i: (0, 0)))
        args.append(bias.reshape(1, D_MODEL))
    return pl.pallas_call(
        functools.partial(_mmr_kernel, merge, bias is not None),
        grid=(n // tm,),
        in_specs=in_specs,
        out_specs=tile,
        out_shape=jax.ShapeDtypeStruct((n, D_MODEL), F32),
        compiler_params=_cparams("parallel"),
        name="mmr_merge" if merge else "mmr",
    )(*args)


SUBLANES = 8


def _conv_shift_slots(ktaps):
    base = HALO - (ktaps - 1)
    shifts = sorted({(base + k) % SUBLANES for k in range(ktaps)} - {0})
    return {b: slot for slot, b in enumerate(shifts)}


def _conv_kernel(mode, ktaps, tm, rc, zero_first, halo_ref, u_ref, w_ref, *rest):
    if mode == "A":
        bdw_ref, lng_ref, lnb_ref, o_ref, cat, shifted = rest
    else:
        bg_ref, o_ref, cat, shifted = rest
    if zero_first:
        first = pl.program_id(1) == 0

        @pl.when(first)
        def _():
            cat[0:HALO, :] = jnp.zeros((HALO, D_MODEL), F32)

        @pl.when(jnp.logical_not(first))
        def _():
            cat[0:HALO, :] = halo_ref[...]
    else:
        cat[0:HALO, :] = halo_ref[...]
    cat[HALO:HALO + tm, :] = u_ref[...]
    base = HALO - (ktaps - 1)
    slots = _conv_shift_slots(ktaps)
    span = tm + HALO - SUBLANES
    for b, slot in slots.items():
        shifted[slot] = cat[b:b + span, :]
    for r0 in range(0, tm, rc):
        acc = None
        for k in range(ktaps):
            b = (base + k) % SUBLANES
            row = r0 + (base + k) - b
            src = cat[row:row + rc, :] if b == 0 else shifted[slots[b], row:row + rc, :]
            term = src * w_ref[k:k + 1, :]
            acc = term if acc is None else acc + term
        if mode == "A":
            y = acc + bdw_ref[...]
            mu = jnp.mean(y, axis=-1, keepdims=True)
            yc = y - mu
            var = jnp.mean(yc * yc, axis=-1, keepdims=True)
            yn = yc * lax.rsqrt(var + NORM_EPS) * lng_ref[...] + lnb_ref[...]
            o_ref[r0:r0 + rc, :] = (yn * jax.nn.sigmoid(yn)).astype(BF16)
        else:
            o_ref[r0:r0 + rc, :] = (bg_ref[r0:r0 + rc, :] * acc).astype(BF16)


def conv_mix(mode, u3, halo3, taps, tm, zero_first, extras):
    nb, t, _ = u3.shape
    ktaps = taps.shape[0]
    kpad = -(-ktaps // 8) * 8
    taps = jnp.pad(taps, ((0, kpad - ktaps), (0, 0)))
    rc = min(32, tm)
    hb = tm // HALO
    tile = pl.BlockSpec((None, tm, D_MODEL), lambda b, i: (b, i, 0))
    row = pl.BlockSpec((1, D_MODEL), lambda b, i: (0, 0))
    if zero_first:
        halo_spec = pl.BlockSpec((None, HALO, D_MODEL), lambda b, i: (b, jnp.maximum(i * hb - 1, 0), 0))
    else:
        halo_spec = pl.BlockSpec((None, HALO, D_MODEL), lambda b, i: (b, 0, 0))
    in_specs = [halo_spec, tile, pl.BlockSpec((kpad, D_MODEL), lambda b, i: (0, 0))]
    args = [halo3, u3, taps]
    if mode == "A":
        in_specs += [row, row, row]
        args += [e.reshape(1, D_MODEL) for e in extras]
    else:
        in_specs += [tile]
        args += list(extras)
    return pl.pallas_call(
        functools.partial(_conv_kernel, mode, ktaps, tm, rc, zero_first),
        grid=(nb, t // tm),
        in_specs=in_specs,
        out_specs=tile,
        out_shape=jax.ShapeDtypeStruct((nb, t, D_MODEL), BF16),
        scratch_shapes=[pltpu.VMEM((HALO + tm, D_MODEL), F32),
                        pltpu.VMEM((len(_conv_shift_slots(ktaps)), tm + HALO - SUBLANES, D_MODEL), F32)],
        compiler_params=_cparams("parallel", "arbitrary"),
        name="conv_" + mode,
    )(*args)


def _dil_attn_kernel(dil, q_ref, k_ref, v_ref, bias_ref, o_ref, lse_ref):
    bq = bias_ref.shape[1]
    n_sub = q_ref.shape[0] // (bq * dil)
    j = pl.program_id(1)
    lane = lax.broadcasted_iota(jnp.int32, (1, LANES), 1)
    lo = lane < HEAD_DIM
    own = (lo, jnp.logical_not(lo))
    for r in range(dil):
        k_prev = v_prev = None
        for sb in range(n_sub):
            rows = pl.ds(r + dil * bq * sb, bq, stride=dil) if dil > 1 else pl.ds(bq * sb, bq)
            q = q_ref[rows, :] * (ATTN_SCALE * LOG2E)
            k_cur = k_ref[rows, :].astype(BF16)
            v_rows = v_ref[rows, :]
            v_cur = [jnp.where(own[e], v_rows, 1.0).astype(BF16) for e in range(2)]
            if sb == 0:
                k, v, cols = k_cur, v_cur, slice(bq, 2 * bq)
            else:
                k = jnp.concatenate([k_prev, k_cur], axis=0)
                v = [jnp.concatenate([v_prev[e], v_cur[e]], axis=0) for e in range(2)]
                cols = slice(0, 2 * bq)
            k_prev, v_prev = k_cur, v_cur
            pvs, ms = [], []
            for e in range(2):
                qe = jnp.where(own[e], q, 0.0).astype(BF16)
                s = _dot_nt(qe, k) + bias_ref[2 * j + e, :, cols]
                m = jnp.max(s, axis=-1, keepdims=True)
                p = jnp.exp2(s - m)
                pvs.append(_dot(p.astype(BF16), v[e]))
                ms.append(m)
            num = jnp.where(lo, pvs[0], pvs[1])
            den = jnp.where(lo, pvs[1], pvs[0])
            den = jnp.concatenate([den[:, HEAD_DIM:], den[:, :HEAD_DIM]], axis=1)
            o_ref[rows, :] = num / den
            lse_ref[rows, :] = jnp.where(lo, ms[0], ms[1]) * (1.0 / LOG2E) + jnp.log(den)


def dilated_prompt_attn(qkv3, bias, g, dil):
    bn, s_len, _ = qkv3.shape
    nlb = D_MODEL // LANES
    ng = len(B_PAIRS)
    spec = lambda which: pl.BlockSpec((None, s_len, LANES), lambda b, j: (b, 0, (which * ng + g) * nlb + j))
    out_spec = pl.BlockSpec((None, s_len, LANES), lambda b, j: (b, 0, j))
    o, lse = pl.pallas_call(
        functools.partial(_dil_attn_kernel, dil),
        grid=(bn, nlb),
        in_specs=[spec(0), spec(1), spec(2), pl.BlockSpec(bias.shape, lambda b, j: (0, 0, 0))],
        out_specs=[out_spec, out_spec],
        out_shape=[jax.ShapeDtypeStruct((bn, s_len, D_MODEL), F32)] * 2,
        compiler_params=_cparams("parallel", "parallel"),
        name="dilated_prompt",
    )(qkv3, qkv3, qkv3, bias)
    return o.reshape(bn * s_len, D_MODEL), lse.reshape(bn * s_len, D_MODEL)


def _head_mask(ncols):
    row = lax.broadcasted_iota(jnp.int32, (N_HEADS * DEC_T, ncols), 0)
    lane = lax.broadcasted_iota(jnp.int32, (N_HEADS * DEC_T, ncols), 1)
    return (row >> 3) == (lane >> 6)


def _build_qbd(q8):
    q = q8 * ATTN_SCALE
    qt = jnp.concatenate([q] * N_HEADS, axis=0)
    return jnp.where(_head_mask(D_MODEL), qt, 0.0).astype(BF16)


def _fold_heads(x):
    xm = jnp.where(_head_mask(D_MODEL), x, 0.0)
    out = xm[0:DEC_T, :]
    for h in range(1, N_HEADS):
        out = out + xm[h * DEC_T:(h + 1) * DEC_T, :]
    return out


def _decode_init(q_ref, new_ref, bnew_ref, qbd_scr, m_scr, l_scr, acc_scr):
    qbd = _build_qbd(q_ref[...])
    qbd_scr[...] = qbd
    kn = new_ref[:, 0:D_MODEL].astype(BF16)
    vn = new_ref[:, D_MODEL:2 * D_MODEL].astype(BF16)
    s = _dot_nt(qbd, kn) + bnew_ref[...]
    m = jnp.max(s, axis=-1, keepdims=True)
    p = jnp.exp(s - m)
    m_scr[...] = m
    l_scr[...] = jnp.sum(p, axis=-1, keepdims=True)
    acc_scr[...] = _dot(p.astype(BF16), vn)


def _decode_update(kt, vt, bias, qbd_scr, m_scr, l_scr, acc_scr):
    s = _dot(qbd_scr[...], kt) + bias
    m_old = m_scr[...]
    m_new = jnp.maximum(m_old, jnp.max(s, axis=-1, keepdims=True))
    alpha = jnp.exp(m_old - m_new)
    p = jnp.exp(s - m_new)
    l_scr[...] = alpha * l_scr[...] + jnp.sum(p, axis=-1, keepdims=True)
    acc_scr[...] = alpha * acc_scr[...] + _dot_nt(p.astype(BF16), vt)
    m_scr[...] = m_new


def _bdec_kernel(q_ref, new_ref, bnew_ref, buf_ref, nxt_ref, newt_ref, bias_ref, o_ref, lse_ref, st_ref,
                 qbd_scr, m_scr, l_scr, acc_scr):
    c = pl.program_id(1)
    last = c == pl.num_programs(1) - 1

    @pl.when(c == 0)
    def _():
        _decode_init(q_ref, new_ref, bnew_ref, qbd_scr, m_scr, l_scr, acc_scr)

    _decode_update(buf_ref[0:D_MODEL, :].astype(BF16), buf_ref[D_MODEL:2 * D_MODEL, :].astype(BF16),
                   bias_ref[...], qbd_scr, m_scr, l_scr, acc_scr)

    width = buf_ref.shape[1]
    rows_blk = 256
    for r0 in range(0, 2 * D_MODEL, rows_blk):
        rs = slice(r0, r0 + rows_blk)
        tail = jnp.where(last, newt_ref[rs, :], nxt_ref[rs, :])
        x = jnp.concatenate([buf_ref[rs, :], tail], axis=1)
        st_ref[rs, :] = x[:, DEC_T:DEC_T + width]

    @pl.when(last)
    def _():
        l = l_scr[...]
        o_ref[...] = _fold_heads(acc_scr[...] / l)
        lse_ref[...] = _fold_heads(jnp.broadcast_to(m_scr[...] + jnp.log(l), (N_HEADS * DEC_T, D_MODEL)))


def dilated_decode(q3, new3, new_t, bias_new, buf_t, bias_buf):
    bn, _, L = buf_t.shape
    rows = N_HEADS * DEC_T
    rchunk = min(L, 512)
    lane_blocks = rchunk // LANES
    per_b = lambda shape: pl.BlockSpec((None,) + shape, lambda b, c: (b, 0, 0))
    o, lse, st = pl.pallas_call(
        _bdec_kernel,
        grid=(bn, L // rchunk),
        in_specs=[
            per_b((DEC_T, D_MODEL)),
            per_b((NEW_PAD, 2 * D_MODEL)),
            pl.BlockSpec((rows, NEW_PAD), lambda b, c: (0, 0)),
            pl.BlockSpec((None, 2 * D_MODEL, rchunk), lambda b, c: (b, 0, c)),
            pl.BlockSpec((None, 2 * D_MODEL, LANES),
                         lambda b, c: (b, 0, jnp.minimum((c + 1) * lane_blocks, L // LANES - 1))),
            per_b((2 * D_MODEL, LANES)),
            pl.BlockSpec((rows, rchunk), lambda b, c: (0, c)),
        ],
        out_specs=[per_b((DEC_T, D_MODEL)), per_b((DEC_T, D_MODEL)),
                   pl.BlockSpec((None, 2 * D_MODEL, rchunk), lambda b, c: (b, 0, c))],
        out_shape=[jax.ShapeDtypeStruct((bn, DEC_T, D_MODEL), F32)] * 2
                  + [jax.ShapeDtypeStruct((bn, 2 * D_MODEL, L), F32)],
        scratch_shapes=[pltpu.VMEM((rows, D_MODEL), BF16), pltpu.VMEM((rows, 1), F32),
                        pltpu.VMEM((rows, 1), F32), pltpu.VMEM((rows, D_MODEL), F32)],
        compiler_params=_cparams("parallel", "arbitrary"),
        name="dilated_decode",
    )(q3, new3, bias_new, buf_t, buf_t, new_t, bias_buf)
    return o.reshape(bn * DEC_T, D_MODEL), lse.reshape(bn * DEC_T, D_MODEL), st


def _fgate_kernel(seg_shift, use_carry, p_ref, b_ref, logf_ref, cum_ref, carry):
    tt = p_ref.shape[0]
    x = p_ref[...] + b_ref[...]
    lf = jnp.minimum(x, 0.0) - jnp.log(1.0 + jnp.exp(-jnp.abs(x)))
    logf_ref[...] = lf
    r = lax.broadcasted_iota(jnp.int32, (tt, tt), 0)
    c = lax.broadcasted_iota(jnp.int32, (tt, tt), 1)
    tri = jnp.where(c <= r, 1.0, 0.0)
    if seg_shift is not None:
        tri = jnp.where((r >> seg_shift) == (c >> seg_shift), tri, 0.0)
    cum = _dot_exact(tri, lf)
    if use_carry:
        @pl.when(pl.program_id(1) == 0)
        def _():
            carry[...] = jnp.zeros_like(carry)

        cum = cum + carry[...]
        carry[...] = cum[tt - 1:tt, :]
    cum_ref[...] = cum


def fgate(p3, b_f, tt, seg_shift, use_carry):
    bn, t, ncols = p3.shape
    fcol = (ncols - LANES) // LANES
    b_pad = jnp.pad(b_f.reshape(1, -1), ((0, 0), (0, LANES - b_f.shape[-1])))
    out_spec = pl.BlockSpec((None, tt, LANES), lambda b, i: (b, i, 0))
    return pl.pallas_call(
        functools.partial(_fgate_kernel, seg_shift, use_carry),
        grid=(bn, t // tt),
        in_specs=[pl.BlockSpec((None, tt, LANES), lambda b, i: (b, i, fcol)),
                  pl.BlockSpec((1, LANES), lambda b, i: (0, 0))],
        out_specs=[out_spec, out_spec],
        out_shape=[jax.ShapeDtypeStruct((bn, t, LANES), F32)] * 2,
        scratch_shapes=[pltpu.VMEM((1, LANES), F32)],
        compiler_params=_cparams("parallel", "arbitrary"),
        name="fgate",
    )(p3, b_pad)


def _split3(x):
    hi = x.astype(BF16)
    r1 = x - hi.astype(F32)
    mid = r1.astype(BF16)
    lo = (r1 - mid.astype(F32)).astype(BF16)
    return jnp.concatenate([hi, mid, lo], axis=1)


def _piece_selector(h, first_lane, sign):
    rho = lax.broadcasted_iota(jnp.int32, (3 * LANES, LANES), 0)
    lam = lax.broadcasted_iota(jnp.int32, (3 * LANES, LANES), 1)
    at_target = jnp.where(lam == first_lane + (rho >> 7), sign, 0.0)
    return jnp.where((rho & (LANES - 1)) == h, at_target, 0.0).astype(BF16)


def _fox_flash_kernel(tq, tk, q_ref, k_ref, v_ref, cum_ref, o_ref, kvt_ref, qaug_scr, kaug, vaug, m_scr, acc_scr):
    hp = pl.program_id(1)
    qi = pl.program_id(2)
    lane = lax.broadcasted_iota(jnp.int32, (1, LANES), 1)
    lo = lane < HEAD_DIM
    own = (lo, jnp.logical_not(lo))
    base = (HEAD_DIM, 0)

    def ones_at(first):
        return jnp.where(lane >= first, jnp.where(lane < first + 3, 1.0, 0.0), 0.0)

    @pl.when(qi == 0)
    def _():
        pieces = _split3(cum_ref[...] * LOG2E)
        qs = q_ref[...] * (ATTN_SCALE * LOG2E)
        for e in range(2):
            k_extras = _dot(pieces, _piece_selector(2 * hp + e, base[e], -1.0)) + ones_at(base[e] + 3)
            q_extras = _dot(pieces, _piece_selector(2 * hp + e, base[e] + 3, 1.0)) + ones_at(base[e])
            kaug[e] = jnp.where(own[e], k_ref[...], k_extras).astype(BF16)
            qaug_scr[e] = jnp.where(own[e], qs, q_extras).astype(BF16)
            vaug[e] = jnp.where(own[e], v_ref[...], 1.0).astype(BF16)
        for pg in range(kvt_ref.shape[0]):
            rs = slice(pg * PAGE, (pg + 1) * PAGE)
            kvt_ref[pg, 0] = k_ref[rs, :].T
            kvt_ref[pg, 1] = v_ref[rs, :].T

    q0 = pl.multiple_of(qi * tq, tq)
    qaug = [qaug_scr[e, pl.ds(q0, tq), :] for e in range(2)]
    m_scr[...] = jnp.full(m_scr.shape, NEG_INF, F32)
    acc_scr[...] = jnp.zeros_like(acc_scr)

    def tile(kt, row0, nrows, causal):
        ks = pl.multiple_of(kt * tk, tk)
        rows = slice(row0, row0 + nrows)
        for e in range(2):
            s = _dot_nt(qaug[e][rows, :], kaug[e, pl.ds(ks, tk), :])
            if causal:
                r = lax.broadcasted_iota(jnp.int32, (nrows, tk), 0)
                c = lax.broadcasted_iota(jnp.int32, (nrows, tk), 1)
                s = jnp.where(r >= c, s, NEG_INF)
            m_old = m_scr[e, rows, :]
            m_new = jnp.maximum(m_old, jnp.max(s, axis=-1, keepdims=True))
            alpha = jnp.exp2(m_old - m_new)
            p = jnp.exp2(s - jnp.concatenate([m_new] * (tk // LANES), axis=1))
            acc_scr[e, rows, :] = alpha * acc_scr[e, rows, :] + _dot(p.astype(BF16), vaug[e, pl.ds(ks, tk), :])
            m_scr[e, rows, :] = m_new

    def body(kt, carry):
        tile(kt, 0, tq, False)
        return carry

    per = tq // tk
    lax.fori_loop(0, qi * per, body, 0)
    for d in range(per):
        tile(qi * per + d, d * tk, tk, True)
        if d + 1 < per:
            tile(qi * per + d, (d + 1) * tk, tq - (d + 1) * tk, False)
    a0, a1 = acc_scr[0], acc_scr[1]
    num = jnp.where(lo, a0, a1)
    den = jnp.where(lo, a1, a0)
    den = jnp.concatenate([den[:, HEAD_DIM:], den[:, :HEAD_DIM]], axis=1)
    o_ref[...] = (num / den).astype(BF16)


def fox_flash(p3, cum3):
    bn, s_len, _ = p3.shape
    tq, tk = 1024, 512
    nh2 = N_HEADS // 2
    n_pg = s_len // PAGE
    return pl.pallas_call(
        functools.partial(_fox_flash_kernel, tq, tk),
        grid=(bn, nh2, s_len // tq),
        in_specs=[
            pl.BlockSpec((None, s_len, LANES), lambda b, h, i: (b, 0, h)),
            pl.BlockSpec((None, s_len, LANES), lambda b, h, i: (b, 0, nh2 + h)),
            pl.BlockSpec((None, s_len, LANES), lambda b, h, i: (b, 0, 2 * nh2 + h)),
            pl.BlockSpec((None, s_len, LANES), lambda b, h, i: (b, 0, 0)),
        ],
        out_specs=[pl.BlockSpec((None, tq, LANES), lambda b, h, i: (b, i, h)),
                   pl.BlockSpec((None, n_pg, 2, None, LANES, PAGE), lambda b, h, i: (b, 0, 0, h, 0, 0))],
        out_shape=[jax.ShapeDtypeStruct((bn, s_len, D_MODEL), BF16),
                   jax.ShapeDtypeStruct((bn, n_pg, 2, nh2, LANES, PAGE), F32)],
        scratch_shapes=[pltpu.VMEM((2, s_len, LANES), BF16)] * 3
                       + [pltpu.VMEM((2, tq, LANES), F32), pltpu.VMEM((2, tq, LANES), F32)],
        compiler_params=_cparams("parallel", "parallel", "arbitrary"),
        name="fox_flash",
    )(p3, p3, p3, cum3)


def _suffix_sums(x):
    lane = lax.broadcasted_iota(jnp.int32, x.shape, 1)
    y = x
    step = 1
    while step < x.shape[1]:
        ahead = pltpu.roll(y, x.shape[1] - step, axis=1)
        y = y + jnp.where(lane < x.shape[1] - step, ahead, 0.0)
        step *= 2
    return y


def _rows_per_head(x):
    return jnp.concatenate([jnp.broadcast_to(x[h:h + 1, :], (DEC_T, x.shape[1])) for h in range(N_HEADS)],
                           axis=0)


def _foxdec_kernel(npg, pt_ref, q_ref, new_ref, cncol_ref, bnew_ref, *rest):
    kv_refs = rest[:npg]
    lf_refs = rest[npg:2 * npg]
    o_ref, qbd_scr, m_scr, l_scr, acc_scr, carry, kcat, vcat = rest[2 * npg:]
    j = pl.program_id(1)

    @pl.when(j == 0)
    def _():
        _decode_init(q_ref, new_ref, bnew_ref, qbd_scr, m_scr, l_scr, acc_scr)
        carry[...] = jnp.zeros_like(carry)

    after = carry[...]
    biases = []
    for i in range(npg):
        lf = lf_refs[i][...]
        incl = _suffix_sums(lf)
        biases.append(after + (incl - lf))
        after = after + incl[:, 0:1]
        kcat[:, i * PAGE:(i + 1) * PAGE] = kv_refs[i][0:D_MODEL, :].astype(BF16)
        vcat[:, i * PAGE:(i + 1) * PAGE] = kv_refs[i][D_MODEL:2 * D_MODEL, :].astype(BF16)
    carry[...] = after
    bias = _rows_per_head(jnp.concatenate(biases, axis=1)) + cncol_ref[...]
    _decode_update(kcat[...], vcat[...], bias, qbd_scr, m_scr, l_scr, acc_scr)

    @pl.when(j == pl.num_programs(1) - 1)
    def _():
        o_ref[...] = _fold_heads(acc_scr[...] / l_scr[...])


def fox_decode(page_table, q3, new3, cn_col, bias_new, cache_kvt, cache_lft, u):
    bn, n_pages = page_table.shape
    npg = 16
    assert n_pages % npg == 0
    rows = N_HEADS * DEC_T

    def page_spec(i, nrows):
        def imap(b, j, pt):
            return (u, pt[b, n_pages - 1 - (j * npg + i)], 0, 0)
        return pl.BlockSpec((None, None, nrows, PAGE), imap)

    per_b = lambda shape: pl.BlockSpec((None,) + shape, lambda b, j, pt: (b, 0, 0))
    grid_spec = pltpu.PrefetchScalarGridSpec(
        num_scalar_prefetch=1,
        grid=(bn, n_pages // npg),
        in_specs=[per_b((DEC_T, D_MODEL)), per_b((NEW_PAD, 2 * D_MODEL)), per_b((rows, 1)),
                  per_b((rows, NEW_PAD))]
                 + [page_spec(i, 2 * D_MODEL) for i in range(npg)]
                 + [page_spec(i, N_HEADS) for i in range(npg)],
        out_specs=per_b((DEC_T, D_MODEL)),
        scratch_shapes=[pltpu.VMEM((rows, D_MODEL), BF16), pltpu.VMEM((rows, 1), F32),
                        pltpu.VMEM((rows, 1), F32), pltpu.VMEM((rows, D_MODEL), F32),
                        pltpu.VMEM((N_HEADS, PAGE), F32),
                        pltpu.VMEM((D_MODEL, npg * PAGE), BF16), pltpu.VMEM((D_MODEL, npg * PAGE), BF16)],
    )
    o = pl.pallas_call(
        functools.partial(_foxdec_kernel, npg),
        grid_spec=grid_spec,
        out_shape=jax.ShapeDtypeStruct((bn, DEC_T, D_MODEL), F32),
        compiler_params=_cparams("parallel", "arbitrary"),
        name="fox_decode",
    )(page_table, q3, new3, cn_col, bias_new, *([cache_kvt] * npg), *([cache_lft] * npg))
    return o.reshape(bn * DEC_T, D_MODEL)


def _t5_bucket(dist):
    max_exact = REL_BUCKETS // 2
    df = jnp.maximum(dist, 1).astype(F32)
    large = max_exact + (jnp.log(df / max_exact) / math.log(REL_MAX_DIST / max_exact)
                         * (REL_BUCKETS - max_exact)).astype(jnp.int32)
    large = jnp.minimum(large, REL_BUCKETS - 1)
    return jnp.where(dist < max_exact, dist, large)


def _bias_table(tab, dist, valid):
    onehot = (_t5_bucket(dist)[..., None] == jnp.arange(REL_BUCKETS)).astype(F32)
    b = jnp.einsum("qkb,bh->hqk", onehot, tab.astype(F32), precision=lax.Precision.HIGHEST)
    return jnp.where(valid[None], b, NEG_INF)


def _prompt_bias(tab, dil, ns, bq):
    steps = jnp.arange(bq)[:, None] - jnp.arange(2 * bq)[None, :] + ns
    valid = (steps >= 0) & (steps <= ns)
    return _bias_table(tab, jnp.clip(steps, 0, ns) * dil, valid)


def _decode_bias(tab, dil, ns, lb):
    t = jnp.arange(DEC_T)

    def table(dist):
        valid = (dist >= 0) & (dist % dil == 0) & (dist <= ns * dil)
        return _bias_table(tab, jnp.maximum(dist, 0), valid).reshape(N_HEADS * DEC_T, dist.shape[1])

    d_buf = lb + t[:, None] - jnp.arange(lb)[None, :]
    s = jnp.arange(NEW_PAD)
    d_new = jnp.where(s[None, :] < DEC_T, t[:, None] - s[None, :], -1)
    return table(d_buf), table(d_new)


def _bcol(which, g):
    c0 = (which * len(B_PAIRS) + g) * D_MODEL
    return slice(c0, c0 + D_MODEL)


def dilated_prompt_stage(qkv3, rel_bias):
    bn, s_len, _ = qkv3.shape
    outs, lses, states = [], [], []
    for g, (win, dil) in enumerate(B_PAIRS):
        ns = win // dil
        assert ns == 128 and s_len % (128 * dil) == 0
        tab = rel_bias[:, g * N_HEADS:(g + 1) * N_HEADS]
        o, lse = dilated_prompt_attn(qkv3, _prompt_bias(tab, dil, ns, 128) * LOG2E, g, dil)
        outs.append(o)
        lses.append(lse)
        lw = min(win, s_len)
        states.append(jnp.stack([qkv3[:, s_len - lw:, _bcol(1, g)], qkv3[:, s_len - lw:, _bcol(2, g)]],
                                axis=2).reshape(bn, lw, 2, N_HEADS, HEAD_DIM))
    return outs, lses, states


def dilated_sample_stage(qkv3, bufs, rel_bias):
    dn, t_new, _ = qkv3.shape
    outs, lses, states = [], [], []
    for g, (win, dil) in enumerate(B_PAIRS):
        ns = win // dil
        tab = rel_bias[:, g * N_HEADS:(g + 1) * N_HEADS]
        lb = bufs[g].shape[1]
        assert lb == ns * dil
        buf_t = bufs[g].transpose(0, 2, 3, 4, 1).reshape(dn, 2 * D_MODEL, lb)
        new_kv = jnp.concatenate([qkv3[:, :, _bcol(1, g)], qkv3[:, :, _bcol(2, g)]], axis=-1)
        bias_buf, bias_new = _decode_bias(tab, dil, ns, lb)
        new_t = jnp.pad(new_kv.transpose(0, 2, 1), ((0, 0), (0, 0), (0, LANES - t_new)))
        o, lse, st = dilated_decode(qkv3[:, :, _bcol(0, g)],
                                    jnp.pad(new_kv, ((0, 0), (0, NEW_PAD - t_new), (0, 0))),
                                    new_t, bias_new, buf_t, bias_buf)
        outs.append(o)
        lses.append(lse)
        states.append(st.reshape(dn, 2, N_HEADS, HEAD_DIM, lb).transpose(0, 4, 1, 2, 3))
    return outs, lses, states


def fox_prompt_stage(p3, b_f):
    bn, s_len, _ = p3.shape
    nf = b_f.shape[-1]
    logf, cum = fgate(p3, b_f, 512, None, True)
    o, kvt = fox_flash(p3, cum)
    kv_pages = kvt.reshape(bn, s_len // PAGE, 2, N_HEADS, HEAD_DIM, PAGE).transpose(0, 1, 5, 2, 3, 4)
    return o, kv_pages, logf[:, :, :nf].reshape(bn, s_len // PAGE, PAGE, nf)


def fox_sample_stage(p3, b_f, page_table, cache_c_kv, cache_c_logf, u):
    dn, t_new, _ = p3.shape
    nf = b_f.shape[-1]
    logf, cn = fgate(p3.reshape(1, dn * t_new, -1), b_f, dn * t_new, 3, False)
    cn_t = cn[0, :, :N_HEADS].reshape(dn, t_new, N_HEADS).transpose(0, 2, 1)
    cn_col = cn_t.reshape(dn, N_HEADS * t_new, 1)
    cn_keys = jnp.broadcast_to(cn_t[:, :, None, :], (dn, N_HEADS, t_new, t_new))
    cn_keys = cn_keys.reshape(dn, N_HEADS * t_new, t_new)
    tq_idx = jnp.tile(jnp.arange(t_new), N_HEADS)[:, None]
    causal = jnp.arange(t_new)[None, :] <= tq_idx
    bias_new = jnp.where(causal[None], cn_col - cn_keys, NEG_INF)
    bias_new = jnp.pad(bias_new, ((0, 0), (0, 0), (0, NEW_PAD - t_new)), constant_values=NEG_INF)
    new_kv = p3[:, :, D_MODEL:3 * D_MODEL]
    nc, n_phys = cache_c_kv.shape[:2]
    kvt = cache_c_kv.transpose(0, 1, 3, 4, 5, 2).reshape(nc, n_phys, 2 * D_MODEL, PAGE)
    lft = cache_c_logf.transpose(0, 1, 3, 2)
    o = fox_decode(page_table, p3[:, :, :D_MODEL],
                   jnp.pad(new_kv, ((0, 0), (0, NEW_PAD - t_new), (0, 0))), cn_col, bias_new, kvt, lft, u)
    return (o, new_kv.reshape(dn, t_new, 2, N_HEADS, HEAD_DIM),
            logf[0, :, :nf].reshape(dn, t_new, nf))


def kernel(x_prompt, x_sample, c_prompt, c_sample, state_a_conv, state_b_kv_w128, state_b_kv_w512, state_b_kv_w2048, cache_c_kv, cache_c_logf, page_table, state_d_conv, ada_w, ada_b, norm_g, final_g, ffn_w13, ffn_w2, a_w_in, a_b_in, a_w_dw, a_b_dw, a_ln_g, a_ln_b, a_w_out, a_b_out, b_w_qkv, b_w_out, rel_bias, c_w_in, c_b_f, c_w_out, d_w_in, d_w_conv, d_w_out):
    bn, s_len, _ = x_prompt.shape
    dn, t_new, _ = x_sample.shape
    assert t_new == DEC_T and s_len % 512 == 0
    n_p, n_s = bn * s_len, dn * t_new
    depth = ada_w.shape[0]
    tm_p, tm_s = 1024, n_s
    tm_ffn = 512
    b_states = (state_b_kv_w128, state_b_kv_w512, state_b_kv_w2048)
    w13_all, w2_all = ffn_w13.astype(BF16), ffn_w2.astype(BF16)

    mod = ada_mod(jnp.concatenate([c_prompt, c_sample], axis=0), ada_w, ada_b)
    mod = mod.reshape(depth, bn + dn, 9, D_MODEL)

    xp = x_prompt.reshape(n_p, D_MODEL)
    xs = x_sample.reshape(n_s, D_MODEL)
    outs = {}
    for l in range(depth):
        kind, u = l % 4, l // 4
        mod_p = mod[l, :bn].reshape(bn, 9, 1, D_MODEL)
        mod_s = jnp.repeat(mod[l, bn:], t_new, axis=0).transpose(1, 0, 2)[None]
        ffn_paths = lambda xp, xs: ((xp, mod_p, tm_ffn), (xs, mod_s, tm_s))

        xp, xs = [ffn(x, m, 0, norm_g[l, 0], w13_all, w2_all, l, 0, tm) for x, m, tm in ffn_paths(xp, xs)]
        paths = ((xp, mod_p, tm_p), (xs, mod_s, tm_s))
        wide_paths = ((xp, mod_p, 2 * tm_p), (xs, mod_s, tm_s))

        if kind == 0:
            w_in, w_out = a_w_in[u].astype(BF16), a_w_out[u].astype(BF16)
            tn = 512
            us = [nmm(x, m, 1, norm_g[l, 1], w_in, (0, D_MODEL // tn), D_MODEL // tn, tn, tm,
                      bias=a_b_in[u], mode="glu") for x, m, tm in paths]
            up3, us3 = us[0].reshape(bn, s_len, D_MODEL), us[1].reshape(dn, t_new, D_MODEL)
            extras = (a_b_dw[u], a_ln_g[u], a_ln_b[u])
            yp = conv_mix("A", up3, up3, a_w_dw[u], 256, True, extras)
            halo_s = jnp.pad(state_a_conv[u], ((0, 0), (HALO - (CONV_A - 1), 0), (0, 0)))
            ys = conv_mix("A", us3, halo_s, a_w_dw[u], t_new, False, extras)
            xp = mmr([yp.reshape(n_p, D_MODEL)], xp, mod_p, 1, w_out, tm_p, bias=a_b_out[u])
            xs = mmr([ys.reshape(n_s, D_MODEL)], xs, mod_s, 1, w_out, tm_s, bias=a_b_out[u])
            outs.setdefault("a_p", []).append(up3[:, s_len - (CONV_A - 1):])
            outs.setdefault("a_s", []).append(
                jnp.concatenate([state_a_conv[u], us3], axis=1)[:, t_new:])
        elif kind == 1:
            w_qkv, w_out = b_w_qkv[u].astype(BF16), b_w_out[u].astype(BF16)
            ncol = w_qkv.shape[1] // 1024
            qkv_p, qkv_s = [nmm(x, m, 1, norm_g[l, 1], w_qkv, (0,), ncol, 1024, tm)
                            for x, m, tm in wide_paths]
            po, pl_, pst = dilated_prompt_stage(qkv_p.reshape(bn, s_len, -1), rel_bias)
            so, sl_, sst = dilated_sample_stage(qkv_s.reshape(dn, t_new, -1),
                                                [b[u] for b in b_states], rel_bias)
            xp = mmr(po + pl_, xp, mod_p, 1, w_out, tm_p // 2)
            xs = mmr(so + sl_, xs, mod_s, 1, w_out, tm_s)
            for g in range(len(B_PAIRS)):
                outs.setdefault("b%d_p" % g, []).append(pst[g])
                outs.setdefault("b%d_s" % g, []).append(sst[g])
        elif kind == 2:
            nf = c_b_f.shape[-1]
            w_in = jnp.pad(c_w_in[u], ((0, 0), (0, LANES - nf))).astype(BF16)
            w_out = c_w_out[u].astype(BF16)
            tn = 640
            ncol = w_in.shape[1] // tn
            pp, ps = [nmm(x, m, 1, norm_g[l, 1], w_in, (0,), ncol, tn, tm) for x, m, tm in wide_paths]
            o_p, kv_p, lf_p = fox_prompt_stage(pp.reshape(bn, s_len, -1), c_b_f[u])
            o_s, kv_s, lf_s = fox_sample_stage(ps.reshape(dn, t_new, -1), c_b_f[u], page_table,
                                               cache_c_kv, cache_c_logf, u)
            xp = mmr([o_p.reshape(n_p, D_MODEL)], xp, mod_p, 1, w_out, tm_p)
            xs = mmr([o_s], xs, mod_s, 1, w_out, tm_s)
            outs.setdefault("ckv_p", []).append(kv_p)
            outs.setdefault("clf_p", []).append(lf_p)
            outs.setdefault("ckv_s", []).append(kv_s)
            outs.setdefault("clf_s", []).append(lf_s)
        else:
            w_in, w_out = d_w_in[u].astype(BF16), d_w_out[u].astype(BF16)
            tn = 512
            nc = D_MODEL // tn
            (bg_p, z_p), (bg_s, z_s) = [
                nmm(x, m, 1, norm_g[l, 1], w_in, (0, nc, 2 * nc), nc, tn, tm, mode="gate3")
                for x, m, tm in paths]
            zp3, zs3 = z_p.reshape(bn, s_len, D_MODEL), z_s.reshape(dn, t_new, D_MODEL)
            yp = conv_mix("D", zp3, zp3, d_w_conv[u], 256, True, (bg_p.reshape(bn, s_len, D_MODEL),))
            halo_s = jnp.pad(state_d_conv[u], ((0, 0), (HALO - (CONV_D - 1), 0), (0, 0)))
            ys = conv_mix("D", zs3, halo_s, d_w_conv[u], t_new, False, (bg_s.reshape(dn, t_new, D_MODEL),))
            xp = mmr([yp.reshape(n_p, D_MODEL)], xp, mod_p, 1, w_out, tm_p)
            xs = mmr([ys.reshape(n_s, D_MODEL)], xs, mod_s, 1, w_out, tm_s)
            outs.setdefault("d_p", []).append(zp3[:, s_len - (CONV_D - 1):])
            outs.setdefault("d_s", []).append(
                jnp.concatenate([state_d_conv[u], zs3], axis=1)[:, t_new:])

        fg = final_g if l == depth - 1 else None
        xp, xs = [ffn(x, m, 2, norm_g[l, 2], w13_all, w2_all, l, 1, tm, final_g=fg)
                  for x, m, tm in ffn_paths(xp, xs)]

    st = lambda key: jnp.stack(outs[key])
    return (xp.reshape(bn, s_len, D_MODEL), xs.reshape(dn, t_new, D_MODEL),
            st("a_p"), st("a_s"), st("b0_p"), st("b0_s"), st("b1_p"), st("b1_s"), st("b2_p"), st("b2_s"),
            st("ckv_p"), st("clf_p"), st("ckv_s"), st("clf_s"), st("d_p"), st("d_s"))
```

```python
import functools
import math

import jax
import jax.numpy as jnp
from jax import lax
from jax.experimental import pallas as pl
from jax.experimental.pallas import tpu as pltpu

F32 = jnp.float32
BF16 = jnp.bfloat16

D_MODEL = 1024
D_FF = 2816
NORM_EPS = 1e-6
HEAD_DIM = 64
N_HEADS = 16
ATTN_SCALE = HEAD_DIM ** -0.5
LOG2E = math.log2(math.e)
NEG_INF = -1e30
CONV_A = 31
CONV_D = 3
B_PAIRS = ((128, 1), (512, 4), (2048, 16))
REL_BUCKETS = 32
REL_MAX_DIST = 2048
PAGE = 128
DEC_T = 8

LANES = 128
HALO = 32
NEW_PAD = 16
VMEM_LIMIT = 56 * 1024 * 1024

NT_DIMS = (((1,), (1,)), ((), ()))


def _cparams(*sem):
    return pltpu.CompilerParams(dimension_semantics=sem, vmem_limit_bytes=VMEM_LIMIT)


def _dot(a, b):
    return jnp.dot(a, b, preferred_element_type=F32)


def _dot_nt(a, b):
    return lax.dot_general(a, b, NT_DIMS, preferred_element_type=F32)


def _dot_exact(a, b):
    return jnp.dot(a, b, preferred_element_type=F32, precision=lax.Precision.HIGHEST)


def _norm_mod(x, g, shift, scale):
    y = x * lax.rsqrt(jnp.mean(x * x, axis=-1, keepdims=True) + NORM_EPS) * g
    return y * (1.0 + scale) + shift


def _ada_kernel(c_ref, w_ref, b_ref, o_ref):
    c = c_ref[...]
    h = (c * jax.nn.sigmoid(c)).astype(BF16)
    o_ref[...] = _dot(h, w_ref[...].astype(BF16)) + b_ref[...]


def ada_mod(c_all, ada_w, ada_b):
    depth, _, n = ada_w.shape
    nb = c_all.shape[0]
    tn = 1024
    return pl.pallas_call(
        _ada_kernel,
        grid=(depth, n // tn),
        in_specs=[
            pl.BlockSpec((nb, D_MODEL), lambda l, j: (0, 0)),
            pl.BlockSpec((None, D_MODEL, tn), lambda l, j: (l, 0, j)),
            pl.BlockSpec((None, 1, tn), lambda l, j: (l, 0, j)),
        ],
        out_specs=pl.BlockSpec((None, nb, tn), lambda l, j: (l, 0, j)),
        out_shape=jax.ShapeDtypeStruct((depth, nb, n), F32),
        compiler_params=_cparams("parallel", "parallel"),
        name="ada_mod",
    )(c_all, ada_w, ada_b.reshape(depth, 1, n))


FFN_CHUNK = 512


def _ffn_kernel(has_final, x_ref, g_ref, sh_ref, sc_ref, gt_ref, w13_ref, w2_ref, *rest):
    if has_final:
        fg_ref, o_ref = rest
    else:
        (o_ref,) = rest
    x = x_ref[...]
    h = _norm_mod(x, g_ref[...], sh_ref[...], sc_ref[...]).astype(BF16)
    acc = None
    for c0 in range(0, D_FF, FFN_CHUNK):
        c1 = min(c0 + FFN_CHUNK, D_FF)
        g = _dot(h, w13_ref[:, c0:c1])
        up = _dot(h, w13_ref[:, D_FF + c0:D_FF + c1])
        a = (g * jax.nn.sigmoid(g) * up).astype(BF16)
        part = _dot(a, w2_ref[c0:c1, :])
        acc = part if acc is None else acc + part
    y = x + (0.5 * gt_ref[...]) * acc
    if has_final:
        y = y * lax.rsqrt(jnp.mean(y * y, axis=-1, keepdims=True) + NORM_EPS) * fg_ref[...]
    o_ref[...] = y


def ffn(x, mod4, j, g_row, w13_all, w2_all, layer, which, tm, final_g=None):
    n = x.shape[0]
    tpb = (n // mod4.shape[0]) // tm
    r = mod4.shape[2]
    mspec = lambda idx: pl.BlockSpec((None, None, r, D_MODEL), lambda i: (i // tpb, idx, 0, 0))
    in_specs = [
        pl.BlockSpec((tm, D_MODEL), lambda i: (i, 0)),
        pl.BlockSpec((1, D_MODEL), lambda i: (0, 0)),
        mspec(3 * j), mspec(3 * j + 1), mspec(3 * j + 2),
        pl.BlockSpec((None, None, D_MODEL, 2 * D_FF), lambda i: (layer, which, 0, 0)),
        pl.BlockSpec((None, None, D_FF, D_MODEL), lambda i: (layer, which, 0, 0)),
    ]
    args = [x, g_row.reshape(1, D_MODEL), mod4, mod4, mod4, w13_all, w2_all]
    if final_g is not None:
        in_specs.append(pl.BlockSpec((1, D_MODEL), lambda i: (0, 0)))
        args.append(final_g.reshape(1, D_MODEL))
    return pl.pallas_call(
        functools.partial(_ffn_kernel, final_g is not None),
        grid=(n // tm,),
        in_specs=in_specs,
        out_specs=pl.BlockSpec((tm, D_MODEL), lambda i: (i, 0)),
        out_shape=jax.ShapeDtypeStruct((n, D_MODEL), F32),
        compiler_params=_cparams("parallel"),
        name="ffn",
    )(*args)


def _nmm_kernel(mode, nw, has_bias, x_ref, g_ref, sh_ref, sc_ref, *rest):
    w_refs = rest[:nw]
    rest = rest[nw:]
    b_refs = rest[:nw] if has_bias else ()
    rest = rest[len(b_refs):]
    nout = 2 if mode == "gate3" else 1
    o_refs = rest[:nout]
    h_scr = rest[nout]

    @pl.when(pl.program_id(1) == 0)
    def _():
        h_scr[...] = _norm_mod(x_ref[...], g_ref[...], sh_ref[...], sc_ref[...]).astype(BF16)

    h = h_scr[...]
    ps = [_dot(h, w[...]) for w in w_refs]
    if has_bias:
        ps = [p + b[...] for p, b in zip(ps, b_refs)]
    if mode == "plain":
        o_refs[0][...] = ps[0]
    elif mode == "glu":
        o_refs[0][...] = ps[0] * jax.nn.sigmoid(ps[1])
    else:
        o_refs[0][...] = ps[0]
        o_refs[1][...] = ps[1] * ps[2]


def nmm(x, mod4, j, g_row, w, col_offsets, ncol, tn, tm, bias=None, mode="plain"):
    n = x.shape[0]
    nw = len(col_offsets)
    tpb = (n // mod4.shape[0]) // tm
    r = mod4.shape[2]
    mspec = lambda idx: pl.BlockSpec((None, None, r, D_MODEL), lambda i, c: (i // tpb, idx, 0, 0))
    in_specs = [
        pl.BlockSpec((tm, D_MODEL), lambda i, c: (i, 0)),
        pl.BlockSpec((1, D_MODEL), lambda i, c: (0, 0)),
        mspec(3 * j), mspec(3 * j + 1),
    ]
    args = [x, g_row.reshape(1, D_MODEL), mod4, mod4]
    for off in col_offsets:
        in_specs.append(pl.BlockSpec((D_MODEL, tn), lambda i, c, off=off: (0, off + c)))
        args.append(w)
    if bias is not None:
        b2 = bias.reshape(1, -1)
        for off in col_offsets:
            in_specs.append(pl.BlockSpec((1, tn), lambda i, c, off=off: (0, off + c)))
            args.append(b2)
    nout = 2 if mode == "gate3" else 1
    out_specs = [pl.BlockSpec((tm, tn), lambda i, c: (i, c)) for _ in range(nout)]
    out_shape = [jax.ShapeDtypeStruct((n, ncol * tn), F32) for _ in range(nout)]
    outs = pl.pallas_call(
        functools.partial(_nmm_kernel, mode, nw, bias is not None),
        grid=(n // tm, ncol),
        in_specs=in_specs,
        out_specs=out_specs,
        out_shape=out_shape,
        scratch_shapes=[pltpu.VMEM((tm, D_MODEL), BF16)],
        compiler_params=_cparams("parallel", "arbitrary"),
        name="nmm_" + mode,
    )(*args)
    return outs if nout == 2 else outs[0]


def _mmr_kernel(merge, has_bias, *refs):
    if merge:
        o1, o2, o3, l1, l2, l3 = refs[:6]
        refs = refs[6:]
        a1, a2, a3 = l1[...], l2[...], l3[...]
        m = jnp.maximum(jnp.maximum(a1, a2), a3)
        e1, e2, e3 = jnp.exp(a1 - m), jnp.exp(a2 - m), jnp.exp(a3 - m)
        inv = 1.0 / (e1 + e2 + e3)
        a = ((e1 * inv) * o1[...] + (e2 * inv) * o2[...] + (e3 * inv) * o3[...]).astype(BF16)
    else:
        a = refs[0][...].astype(BF16)
        refs = refs[1:]
    if has_bias:
        x_ref, gt_ref, w_ref, b_ref, o_ref = refs
    else:
        x_ref, gt_ref, w_ref, o_ref = refs
    y = _dot(a, w_ref[...])
    if has_bias:
        y = y + b_ref[...]
    o_ref[...] = x_ref[...] + gt_ref[...] * y


def mmr(a_list, x, mod4, j, w, tm, bias=None):
    n = x.shape[0]
    merge = len(a_list) == 6
    tpb = (n // mod4.shape[0]) // tm
    r = mod4.shape[2]
    tile = pl.BlockSpec((tm, D_MODEL), lambda i: (i, 0))
    in_specs = [tile for _ in a_list] + [
        tile,
        pl.BlockSpec((None, None, r, D_MODEL), lambda i: (i // tpb, 3 * j + 2, 0, 0)),
        pl.BlockSpec((D_MODEL, D_MODEL), lambda i: (0, 0)),
    ]
    args = list(a_list) + [x, mod4, w]
    if bias is not None:
        in_specs.append(pl.BlockSpec((1, D_MODEL), lambda i: (0, 0)))
        args.append(bias.reshape(1, D_MODEL))
    return pl.pallas_call(
        functools.partial(_mmr_kernel, merge, bias is not None),
        grid=(n // tm,),
        in_specs=in_specs,
        out_specs=tile,
        out_shape=jax.ShapeDtypeStruct((n, D_MODEL), F32),
        compiler_params=_cparams("parallel"),
        name="mmr_merge" if merge else "mmr",
    )(*args)


SUBLANES = 8


def _conv_shift_slots(ktaps):
    base = HALO - (ktaps - 1)
    shifts = sorted({(base + k) % SUBLANES for k in range(ktaps)} - {0})
    return {b: slot for slot, b in enumerate(shifts)}


def _conv_kernel(mode, ktaps, tm, rc, zero_first, halo_ref, u_ref, w_ref, *rest):
    if mode == "A":
        bdw_ref, lng_ref, lnb_ref, o_ref, cat, shifted = rest
    else:
        bg_ref, o_ref, cat, shifted = rest
    if zero_first:
        first = pl.program_id(1) == 0

        @pl.when(first)
        def _():
            cat[0:HALO, :] = jnp.zeros((HALO, D_MODEL), F32)

        @pl.when(jnp.logical_not(first))
        def _():
            cat[0:HALO, :] = halo_ref[...]
    else:
        cat[0:HALO, :] = halo_ref[...]
    cat[HALO:HALO + tm, :] = u_ref[...]
    base = HALO - (ktaps - 1)
    slots = _conv_shift_slots(ktaps)
    span = tm + HALO - SUBLANES
    for b, slot in slots.items():
        shifted[slot] = cat[b:b + span, :]
    for r0 in range(0, tm, rc):
        acc = None
        for k in range(ktaps):
            b = (base + k) % SUBLANES
            row = r0 + (base + k) - b
            src = cat[row:row + rc, :] if b == 0 else shifted[slots[b], row:row + rc, :]
            term = src * w_ref[k:k + 1, :]
            acc = term if acc is None else acc + term
        if mode == "A":
            y = acc + bdw_ref[...]
            mu = jnp.mean(y, axis=-1, keepdims=True)
            yc = y - mu
            var = jnp.mean(yc * yc, axis=-1, keepdims=True)
            yn = yc * lax.rsqrt(var + NORM_EPS) * lng_ref[...] + lnb_ref[...]
            o_ref[r0:r0 + rc, :] = (yn * jax.nn.sigmoid(yn)).astype(BF16)
        else:
            o_ref[r0:r0 + rc, :] = (bg_ref[r0:r0 + rc, :] * acc).astype(BF16)


def conv_mix(mode, u3, halo3, taps, tm, zero_first, extras):
    nb, t, _ = u3.shape
    ktaps = taps.shape[0]
    kpad = -(-ktaps // 8) * 8
    taps = jnp.pad(taps, ((0, kpad - ktaps), (0, 0)))
    rc = min(32, tm)
    hb = tm // HALO
    tile = pl.BlockSpec((None, tm, D_MODEL), lambda b, i: (b, i, 0))
    row = pl.BlockSpec((1, D_MODEL), lambda b, i: (0, 0))
    if zero_first:
        halo_spec = pl.BlockSpec((None, HALO, D_MODEL), lambda b, i: (b, jnp.maximum(i * hb - 1, 0), 0))
    else:
        halo_spec = pl.BlockSpec((None, HALO, D_MODEL), lambda b, i: (b, 0, 0))
    in_specs = [halo_spec, tile, pl.BlockSpec((kpad, D_MODEL), lambda b, i: (0, 0))]
    args = [halo3, u3, taps]
    if mode == "A":
        in_specs += [row, row, row]
        args += [e.reshape(1, D_MODEL) for e in extras]
    else:
        in_specs += [tile]
        args += list(extras)
    return pl.pallas_call(
        functools.partial(_conv_kernel, mode, ktaps, tm, rc, zero_first),
        grid=(nb, t // tm),
        in_specs=in_specs,
        out_specs=tile,
        out_shape=jax.ShapeDtypeStruct((nb, t, D_MODEL), BF16),
        scratch_shapes=[pltpu.VMEM((HALO + tm, D_MODEL), F32),
                        pltpu.VMEM((len(_conv_shift_slots(ktaps)), tm + HALO - SUBLANES, D_MODEL), F32)],
        compiler_params=_cparams("parallel", "arbitrary"),
        name="conv_" + mode,
    )(*args)


def _dil_attn_kernel(dil, q_ref, k_ref, v_ref, bias_ref, o_ref, lse_ref):
    bq = bias_ref.shape[1]
    n_sub = q_ref.shape[0] // (bq * dil)
    j = pl.program_id(1)
    lane = lax.broadcasted_iota(jnp.int32, (1, LANES), 1)
    lo = lane < HEAD_DIM
    own = (lo, jnp.logical_not(lo))
    for r in range(dil):
        k_prev = v_prev = None
        for sb in range(n_sub):
            rows = pl.ds(r + dil * bq * sb, bq, stride=dil) if dil > 1 else pl.ds(bq * sb, bq)
            q = q_ref[rows, :] * (ATTN_SCALE * LOG2E)
            k_cur = k_ref[rows, :].astype(BF16)
            v_rows = v_ref[rows, :]
            v_cur = [jnp.where(own[e], v_rows, 1.0).astype(BF16) for e in range(2)]
            if sb == 0:
                k, v, cols = k_cur, v_cur, slice(bq, 2 * bq)
            else:
                k = jnp.concatenate([k_prev, k_cur], axis=0)
                v = [jnp.concatenate([v_prev[e], v_cur[e]], axis=0) for e in range(2)]
                cols = slice(0, 2 * bq)
            k_prev, v_prev = k_cur, v_cur
            pvs, ms = [], []
            for e in range(2):
                qe = jnp.where(own[e], q, 0.0).astype(BF16)
                s = _dot_nt(qe, k) + bias_ref[2 * j + e, :, cols]
                m = jnp.max(s, axis=-1, keepdims=True)
                p = jnp.exp2(s - m)
                pvs.append(_dot(p.astype(BF16), v[e]))
                ms.append(m)
            num = jnp.where(lo, pvs[0], pvs[1])
            den = jnp.where(lo, pvs[1], pvs[0])
            den = jnp.concatenate([den[:, HEAD_DIM:], den[:, :HEAD_DIM]], axis=1)
            o_ref[rows, :] = num / den
            lse_ref[rows, :] = jnp.where(lo, ms[0], ms[1]) * (1.0 / LOG2E) + jnp.log(den)


def dilated_prompt_attn(qkv3, bias, g, dil):
    bn, s_len, _ = qkv3.shape
    nlb = D_MODEL // LANES
    ng = len(B_PAIRS)
    spec = lambda which: pl.BlockSpec((None, s_len, LANES), lambda b, j: (b, 0, (which * ng + g) * nlb + j))
    out_spec = pl.BlockSpec((None, s_len, LANES), lambda b, j: (b, 0, j))
    o, lse = pl.pallas_call(
        functools.partial(_dil_attn_kernel, dil),
        grid=(bn, nlb),
        in_specs=[spec(0), spec(1), spec(2), pl.BlockSpec(bias.shape, lambda b, j: (0, 0, 0))],
        out_specs=[out_spec, out_spec],
        out_shape=[jax.ShapeDtypeStruct((bn, s_len, D_MODEL), F32)] * 2,
        compiler_params=_cparams("parallel", "parallel"),
        name="dilated_prompt",
    )(qkv3, qkv3, qkv3, bias)
    return o.reshape(bn * s_len, D_MODEL), lse.reshape(bn * s_len, D_MODEL)


def _head_mask(ncols):
    row = lax.broadcasted_iota(jnp.int32, (N_HEADS * DEC_T, ncols), 0)
    lane = lax.broadcasted_iota(jnp.int32, (N_HEADS * DEC_T, ncols), 1)
    return (row >> 3) == (lane >> 6)


def _build_qbd(q8):
    q = q8 * ATTN_SCALE
    qt = jnp.concatenate([q] * N_HEADS, axis=0)
    return jnp.where(_head_mask(D_MODEL), qt, 0.0).astype(BF16)


def _fold_heads(x):
    xm = jnp.where(_head_mask(D_MODEL), x, 0.0)
    out = xm[0:DEC_T, :]
    for h in range(1, N_HEADS):
        out = out + xm[h * DEC_T:(h + 1) * DEC_T, :]
    return out


def _decode_init(q_ref, new_ref, bnew_ref, qbd_scr, m_scr, l_scr, acc_scr):
    qbd = _build_qbd(q_ref[...])
    qbd_scr[...] = qbd
    kn = new_ref[:, 0:D_MODEL].astype(BF16)
    vn = new_ref[:, D_MODEL:2 * D_MODEL].astype(BF16)
    s = _dot_nt(qbd, kn) + bnew_ref[...]
    m = jnp.max(s, axis=-1, keepdims=True)
    p = jnp.exp(s - m)
    m_scr[...] = m
    l_scr[...] = jnp.sum(p, axis=-1, keepdims=True)
    acc_scr[...] = _dot(p.astype(BF16), vn)


def _decode_update(kt, vt, bias, qbd_scr, m_scr, l_scr, acc_scr):
    s = _dot(qbd_scr[...], kt) + bias
    m_old = m_scr[...]
    m_new = jnp.maximum(m_old, jnp.max(s, axis=-1, keepdims=True))
    alpha = jnp.exp(m_old - m_new)
    p = jnp.exp(s - m_new)
    l_scr[...] = alpha * l_scr[...] + jnp.sum(p, axis=-1, keepdims=True)
    acc_scr[...] = alpha * acc_scr[...] + _dot_nt(p.astype(BF16), vt)
    m_scr[...] = m_new


def _bdec_kernel(q_ref, new_ref, bnew_ref, buf_ref, nxt_ref, newt_ref, bias_ref, o_ref, lse_ref, st_ref,
                 qbd_scr, m_scr, l_scr, acc_scr):
    c = pl.program_id(1)
    last = c == pl.num_programs(1) - 1

    @pl.when(c == 0)
    def _():
        _decode_init(q_ref, new_ref, bnew_ref, qbd_scr, m_scr, l_scr, acc_scr)

    _decode_update(buf_ref[0:D_MODEL, :].astype(BF16), buf_ref[D_MODEL:2 * D_MODEL, :].astype(BF16),
                   bias_ref[...], qbd_scr, m_scr, l_scr, acc_scr)

    width = buf_ref.shape[1]
    rows_blk = 256
    for r0 in range(0, 2 * D_MODEL, rows_blk):
        rs = slice(r0, r0 + rows_blk)
        tail = jnp.where(last, newt_ref[rs, :], nxt_ref[rs, :])
        x = jnp.concatenate([buf_ref[rs, :], tail], axis=1)
        st_ref[rs, :] = x[:, DEC_T:DEC_T + width]

    @pl.when(last)
    def _():
        l = l_scr[...]
        o_ref[...] = _fold_heads(acc_scr[...] / l)
        lse_ref[...] = _fold_heads(jnp.broadcast_to(m_scr[...] + jnp.log(l), (N_HEADS * DEC_T, D_MODEL)))


def dilated_decode(q3, new3, new_t, bias_new, buf_t, bias_buf):
    bn, _, L = buf_t.shape
    rows = N_HEADS * DEC_T
    rchunk = min(L, 512)
    lane_blocks = rchunk // LANES
    per_b = lambda shape: pl.BlockSpec((None,) + shape, lambda b, c: (b, 0, 0))
    o, lse, st = pl.pallas_call(
        _bdec_kernel,
        grid=(bn, L // rchunk),
        in_specs=[
            per_b((DEC_T, D_MODEL)),
            per_b((NEW_PAD, 2 * D_MODEL)),
            pl.BlockSpec((rows, NEW_PAD), lambda b, c: (0, 0)),
            pl.BlockSpec((None, 2 * D_MODEL, rchunk), lambda b, c: (b, 0, c)),
            pl.BlockSpec((None, 2 * D_MODEL, LANES),
                         lambda b, c: (b, 0, jnp.minimum((c + 1) * lane_blocks, L // LANES - 1))),
            per_b((2 * D_MODEL, LANES)),
            pl.BlockSpec((rows, rchunk), lambda b, c: (0, c)),
        ],
        out_specs=[per_b((DEC_T, D_MODEL)), per_b((DEC_T, D_MODEL)),
                   pl.BlockSpec((None, 2 * D_MODEL, rchunk), lambda b, c: (b, 0, c))],
        out_shape=[jax.ShapeDtypeStruct((bn, DEC_T, D_MODEL), F32)] * 2
                  + [jax.ShapeDtypeStruct((bn, 2 * D_MODEL, L), F32)],
        scratch_shapes=[pltpu.VMEM((rows, D_MODEL), BF16), pltpu.VMEM((rows, 1), F32),
                        pltpu.VMEM((rows, 1), F32), pltpu.VMEM((rows, D_MODEL), F32)],
        compiler_params=_cparams("parallel", "arbitrary"),
        name="dilated_decode",
    )(q3, new3, bias_new, buf_t, buf_t, new_t, bias_buf)
    return o.reshape(bn * DEC_T, D_MODEL), lse.reshape(bn * DEC_T, D_MODEL), st


def _fgate_kernel(seg_shift, use_carry, p_ref, b_ref, logf_ref, cum_ref, carry):
    tt = p_ref.shape[0]
    x = p_ref[...] + b_ref[...]
    lf = jnp.minimum(x, 0.0) - jnp.log(1.0 + jnp.exp(-jnp.abs(x)))
    logf_ref[...] = lf
    r = lax.broadcasted_iota(jnp.int32, (tt, tt), 0)
    c = lax.broadcasted_iota(jnp.int32, (tt, tt), 1)
    tri = jnp.where(c <= r, 1.0, 0.0)
    if seg_shift is not None:
        tri = jnp.where((r >> seg_shift) == (c >> seg_shift), tri, 0.0)
    cum = _dot_exact(tri, lf)
    if use_carry:
        @pl.when(pl.program_id(1) == 0)
        def _():
            carry[...] = jnp.zeros_like(carry)

        cum = cum + carry[...]
        carry[...] = cum[tt - 1:tt, :]
    cum_ref[...] = cum


def fgate(p3, b_f, tt, seg_shift, use_carry):
    bn, t, ncols = p3.shape
    fcol = (ncols - LANES) // LANES
    b_pad = jnp.pad(b_f.reshape(1, -1), ((0, 0), (0, LANES - b_f.shape[-1])))
    out_spec = pl.BlockSpec((None, tt, LANES), lambda b, i: (b, i, 0))
    return pl.pallas_call(
        functools.partial(_fgate_kernel, seg_shift, use_carry),
        grid=(bn, t // tt),
        in_specs=[pl.BlockSpec((None, tt, LANES), lambda b, i: (b, i, fcol)),
                  pl.BlockSpec((1, LANES), lambda b, i: (0, 0))],
        out_specs=[out_spec, out_spec],
        out_shape=[jax.ShapeDtypeStruct((bn, t, LANES), F32)] * 2,
        scratch_shapes=[pltpu.VMEM((1, LANES), F32)],
        compiler_params=_cparams("parallel", "arbitrary"),
        name="fgate",
    )(p3, b_pad)


def _split3(x):
    hi = x.astype(BF16)
    r1 = x - hi.astype(F32)
    mid = r1.astype(BF16)
    lo = (r1 - mid.astype(F32)).astype(BF16)
    return jnp.concatenate([hi, mid, lo], axis=1)


def _piece_selector(h, first_lane, sign):
    rho = lax.broadcasted_iota(jnp.int32, (3 * LANES, LANES), 0)
    lam = lax.broadcasted_iota(jnp.int32, (3 * LANES, LANES), 1)
    at_target = jnp.where(lam == first_lane + (rho >> 7), sign, 0.0)
    return jnp.where((rho & (LANES - 1)) == h, at_target, 0.0).astype(BF16)


def _fox_flash_kernel(tq, tk, q_ref, k_ref, v_ref, cum_ref, o_ref, kvt_ref, qaug_scr, kaug, vaug, m_scr, acc_scr):
    hp = pl.program_id(1)
    qi = pl.program_id(2)
    lane = lax.broadcasted_iota(jnp.int32, (1, LANES), 1)
    lo = lane < HEAD_DIM
    own = (lo, jnp.logical_not(lo))
    base = (HEAD_DIM, 0)

    def ones_at(first):
        return jnp.where(lane >= first, jnp.where(lane < first + 3, 1.0, 0.0), 0.0)

    @pl.when(qi == 0)
    def _():
        pieces = _split3(cum_ref[...] * LOG2E)
        qs = q_ref[...] * (ATTN_SCALE * LOG2E)
        for e in range(2):
            k_extras = _dot(pieces, _piece_selector(2 * hp + e, base[e], -1.0)) + ones_at(base[e] + 3)
            q_extras = _dot(pieces, _piece_selector(2 * hp + e, base[e] + 3, 1.0)) + ones_at(base[e])
            kaug[e] = jnp.where(own[e], k_ref[...], k_extras).astype(BF16)
            qaug_scr[e] = jnp.where(own[e], qs, q_extras).astype(BF16)
            vaug[e] = jnp.where(own[e], v_ref[...], 1.0).astype(BF16)
        for pg in range(kvt_ref.shape[0]):
            rs = slice(pg * PAGE, (pg + 1) * PAGE)
            kvt_ref[pg, 0] = k_ref[rs, :].T
            kvt_ref[pg, 1] = v_ref[rs, :].T

    q0 = pl.multiple_of(qi * tq, tq)
    qaug = [qaug_scr[e, pl.ds(q0, tq), :] for e in range(2)]
    m_scr[...] = jnp.full(m_scr.shape, NEG_INF, F32)
    acc_scr[...] = jnp.zeros_like(acc_scr)

    def tile(kt, row0, nrows, causal):
        ks = pl.multiple_of(kt * tk, tk)
        rows = slice(row0, row0 + nrows)
        for e in range(2):
            s = _dot_nt(qaug[e][rows, :], kaug[e, pl.ds(ks, tk), :])
            if causal:
                r = lax.broadcasted_iota(jnp.int32, (nrows, tk), 0)
                c = lax.broadcasted_iota(jnp.int32, (nrows, tk), 1)
                s = jnp.where(r >= c, s, NEG_INF)
            m_old = m_scr[e, rows, :]
            m_new = jnp.maximum(m_old, jnp.max(s, axis=-1, keepdims=True))
            alpha = jnp.exp2(m_old - m_new)
            p = jnp.exp2(s - jnp.concatenate([m_new] * (tk // LANES), axis=1))
            acc_scr[e, rows, :] = alpha * acc_scr[e, rows, :] + _dot(p.astype(BF16), vaug[e, pl.ds(ks, tk), :])
            m_scr[e, rows, :] = m_new

    def body(kt, carry):
        tile(kt, 0, tq, False)
        return carry

    per = tq // tk
    lax.fori_loop(0, qi * per, body, 0)
    for d in range(per):
        tile(qi * per + d, d * tk, tk, True)
        if d + 1 < per:
            tile(qi * per + d, (d + 1) * tk, tq - (d + 1) * tk, False)
    a0, a1 = acc_scr[0], acc_scr[1]
    num = jnp.where(lo, a0, a1)
    den = jnp.where(lo, a1, a0)
    den = jnp.concatenate([den[:, HEAD_DIM:], den[:, :HEAD_DIM]], axis=1)
    o_ref[...] = (num / den).astype(BF16)


def fox_flash(p3, cum3):
    bn, s_len, _ = p3.shape
    tq, tk = 2048, 1024
    assert s_len % tq == 0 and tq % tk == 0
    nh2 = N_HEADS // 2
    n_pg = s_len // PAGE
    return pl.pallas_call(
        functools.partial(_fox_flash_kernel, tq, tk),
        grid=(bn, nh2, s_len // tq),
        in_specs=[
            pl.BlockSpec((None, s_len, LANES), lambda b, h, i: (b, 0, h)),
            pl.BlockSpec((None, s_len, LANES), lambda b, h, i: (b, 0, nh2 + h)),
            pl.BlockSpec((None, s_len, LANES), lambda b, h, i: (b, 0, 2 * nh2 + h)),
            pl.BlockSpec((None, s_len, LANES), lambda b, h, i: (b, 0, 0)),
        ],
        out_specs=[pl.BlockSpec((None, tq, LANES), lambda b, h, i: (b, i, h)),
                   pl.BlockSpec((None, n_pg, 2, None, LANES, PAGE), lambda b, h, i: (b, 0, 0, h, 0, 0))],
        out_shape=[jax.ShapeDtypeStruct((bn, s_len, D_MODEL), BF16),
                   jax.ShapeDtypeStruct((bn, n_pg, 2, nh2, LANES, PAGE), F32)],
        scratch_shapes=[pltpu.VMEM((2, s_len, LANES), BF16)] * 3
                       + [pltpu.VMEM((2, tq, LANES), F32), pltpu.VMEM((2, tq, LANES), F32)],
        compiler_params=_cparams("parallel", "parallel", "arbitrary"),
        name="fox_flash",
    )(p3, p3, p3, cum3)


def _suffix_sums(x):
    lane = lax.broadcasted_iota(jnp.int32, x.shape, 1)
    y = x
    step = 1
    while step < x.shape[1]:
        ahead = pltpu.roll(y, x.shape[1] - step, axis=1)
        y = y + jnp.where(lane < x.shape[1] - step, ahead, 0.0)
        step *= 2
    return y


def _rows_per_head(x):
    return jnp.concatenate([jnp.broadcast_to(x[h:h + 1, :], (DEC_T, x.shape[1])) for h in range(N_HEADS)],
                           axis=0)


def _foxdec_kernel(npg, pt_ref, q_ref, new_ref, cncol_ref, bnew_ref, *rest):
    kv_refs = rest[:npg]
    lf_refs = rest[npg:2 * npg]
    o_ref, qbd_scr, m_scr, l_scr, acc_scr, carry, kcat, vcat = rest[2 * npg:]
    j = pl.program_id(1)

    @pl.when(j == 0)
    def _():
        _decode_init(q_ref, new_ref, bnew_ref, qbd_scr, m_scr, l_scr, acc_scr)
        carry[...] = jnp.zeros_like(carry)

    after = carry[...]
    biases = []
    for i in range(npg):
        lf = lf_refs[i][...]
        incl = _suffix_sums(lf)
        biases.append(after + (incl - lf))
        after = after + incl[:, 0:1]
        kcat[:, i * PAGE:(i + 1) * PAGE] = kv_refs[i][0:D_MODEL, :].astype(BF16)
        vcat[:, i * PAGE:(i + 1) * PAGE] = kv_refs[i][D_MODEL:2 * D_MODEL, :].astype(BF16)
    carry[...] = after
    bias = _rows_per_head(jnp.concatenate(biases, axis=1)) + cncol_ref[...]
    _decode_update(kcat[...], vcat[...], bias, qbd_scr, m_scr, l_scr, acc_scr)

    @pl.when(j == pl.num_programs(1) - 1)
    def _():
        o_ref[...] = _fold_heads(acc_scr[...] / l_scr[...])


def fox_decode(page_table, q3, new3, cn_col, bias_new, cache_kvt, cache_lft, u):
    bn, n_pages = page_table.shape
    npg = 16
    assert n_pages % npg == 0
    rows = N_HEADS * DEC_T

    def page_spec(i, nrows):
        def imap(b, j, pt):
            return (u, pt[b, n_pages - 1 - (j * npg + i)], 0, 0)
        return pl.BlockSpec((None, None, nrows, PAGE), imap)

    per_b = lambda shape: pl.BlockSpec((None,) + shape, lambda b, j, pt: (b, 0, 0))
    grid_spec = pltpu.PrefetchScalarGridSpec(
        num_scalar_prefetch=1,
        grid=(bn, n_pages // npg),
        in_specs=[per_b((DEC_T, D_MODEL)), per_b((NEW_PAD, 2 * D_MODEL)), per_b((rows, 1)),
                  per_b((rows, NEW_PAD))]
                 + [page_spec(i, 2 * D_MODEL) for i in range(npg)]
                 + [page_spec(i, N_HEADS) for i in range(npg)],
        out_specs=per_b((DEC_T, D_MODEL)),
        scratch_shapes=[pltpu.VMEM((rows, D_MODEL), BF16), pltpu.VMEM((rows, 1), F32),
                        pltpu.VMEM((rows, 1), F32), pltpu.VMEM((rows, D_MODEL), F32),
                        pltpu.VMEM((N_HEADS, PAGE), F32),
                        pltpu.VMEM((D_MODEL, npg * PAGE), BF16), pltpu.VMEM((D_MODEL, npg * PAGE), BF16)],
    )
    o = pl.pallas_call(
        functools.partial(_foxdec_kernel, npg),
        grid_spec=grid_spec,
        out_shape=jax.ShapeDtypeStruct((bn, DEC_T, D_MODEL), F32),
        compiler_params=_cparams("parallel", "arbitrary"),
        name="fox_decode",
    )(page_table, q3, new3, cn_col, bias_new, *([cache_kvt] * npg), *([cache_lft] * npg))
    return o.reshape(bn * DEC_T, D_MODEL)


def _t5_bucket(dist):
    max_exact = REL_BUCKETS // 2
    df = jnp.maximum(dist, 1).astype(F32)
    large = max_exact + (jnp.log(df / max_exact) / math.log(REL_MAX_DIST / max_exact)
                         * (REL_BUCKETS - max_exact)).astype(jnp.int32)
    large = jnp.minimum(large, REL_BUCKETS - 1)
    return jnp.where(dist < max_exact, dist, large)


def _bias_table(tab, dist, valid):
    onehot = (_t5_bucket(dist)[..., None] == jnp.arange(REL_BUCKETS)).astype(F32)
    b = jnp.einsum("qkb,bh->hqk", onehot, tab.astype(F32), precision=lax.Precision.HIGHEST)
    return jnp.where(valid[None], b, NEG_INF)


def _prompt_bias(tab, dil, ns, bq):
    steps = jnp.arange(bq)[:, None] - jnp.arange(2 * bq)[None, :] + ns
    valid = (steps >= 0) & (steps <= ns)
    return _bias_table(tab, jnp.clip(steps, 0, ns) * dil, valid)


def _decode_bias(tab, dil, ns, lb):
    t = jnp.arange(DEC_T)

    def table(dist):
        valid = (dist >= 0) & (dist % dil == 0) & (dist <= ns * dil)
        return _bias_table(tab, jnp.maximum(dist, 0), valid).reshape(N_HEADS * DEC_T, dist.shape[1])

    d_buf = lb + t[:, None] - jnp.arange(lb)[None, :]
    s = jnp.arange(NEW_PAD)
    d_new = jnp.where(s[None, :] < DEC_T, t[:, None] - s[None, :], -1)
    return table(d_buf), table(d_new)


def _bcol(which, g):
    c0 = (which * len(B_PAIRS) + g) * D_MODEL
    return slice(c0, c0 + D_MODEL)


def dilated_prompt_stage(qkv3, rel_bias):
    bn, s_len, _ = qkv3.shape
    outs, lses, states = [], [], []
    for g, (win, dil) in enumerate(B_PAIRS):
        ns = win // dil
        assert ns == 128 and s_len % (128 * dil) == 0
        tab = rel_bias[:, g * N_HEADS:(g + 1) * N_HEADS]
        o, lse = dilated_prompt_attn(qkv3, _prompt_bias(tab, dil, ns, 128) * LOG2E, g, dil)
        outs.append(o)
        lses.append(lse)
        lw = min(win, s_len)
        states.append(jnp.stack([qkv3[:, s_len - lw:, _bcol(1, g)], qkv3[:, s_len - lw:, _bcol(2, g)]],
                                axis=2).reshape(bn, lw, 2, N_HEADS, HEAD_DIM))
    return outs, lses, states


def dilated_sample_stage(qkv3, bufs, rel_bias):
    dn, t_new, _ = qkv3.shape
    outs, lses, states = [], [], []
    for g, (win, dil) in enumerate(B_PAIRS):
        ns = win // dil
        tab = rel_bias[:, g * N_HEADS:(g + 1) * N_HEADS]
        lb = bufs[g].shape[1]
        assert lb == ns * dil
        buf_t = bufs[g].transpose(0, 2, 3, 4, 1).reshape(dn, 2 * D_MODEL, lb)
        new_kv = jnp.concatenate([qkv3[:, :, _bcol(1, g)], qkv3[:, :, _bcol(2, g)]], axis=-1)
        bias_buf, bias_new = _decode_bias(tab, dil, ns, lb)
        new_t = jnp.pad(new_kv.transpose(0, 2, 1), ((0, 0), (0, 0), (0, LANES - t_new)))
        o, lse, st = dilated_decode(qkv3[:, :, _bcol(0, g)],
                                    jnp.pad(new_kv, ((0, 0), (0, NEW_PAD - t_new), (0, 0))),
                                    new_t, bias_new, buf_t, bias_buf)
        outs.append(o)
        lses.append(lse)
        states.append(st.reshape(dn, 2, N_HEADS, HEAD_DIM, lb).transpose(0, 4, 1, 2, 3))
    return outs, lses, states


def fox_prompt_stage(p3, b_f):
    bn, s_len, _ = p3.shape
    nf = b_f.shape[-1]
    logf, cum = fgate(p3, b_f, 512, None, True)
    o, kvt = fox_flash(p3, cum)
    kv_pages = kvt.reshape(bn, s_len // PAGE, 2, N_HEADS, HEAD_DIM, PAGE).transpose(0, 1, 5, 2, 3, 4)
    return o, kv_pages, logf[:, :, :nf].reshape(bn, s_len // PAGE, PAGE, nf)


def fox_sample_stage(p3, b_f, page_table, cache_c_kv, cache_c_logf, u):
    dn, t_new, _ = p3.shape
    nf = b_f.shape[-1]
    logf, cn = fgate(p3.reshape(1, dn * t_new, -1), b_f, dn * t_new, 3, False)
    cn_t = cn[0, :, :N_HEADS].reshape(dn, t_new, N_HEADS).transpose(0, 2, 1)
    cn_col = cn_t.reshape(dn, N_HEADS * t_new, 1)
    cn_keys = jnp.broadcast_to(cn_t[:, :, None, :], (dn, N_HEADS, t_new, t_new))
    cn_keys = cn_keys.reshape(dn, N_HEADS * t_new, t_new)
    tq_idx = jnp.tile(jnp.arange(t_new), N_HEADS)[:, None]
    causal = jnp.arange(t_new)[None, :] <= tq_idx
    bias_new = jnp.where(causal[None], cn_col - cn_keys, NEG_INF)
    bias_new = jnp.pad(bias_new, ((0, 0), (0, 0), (0, NEW_PAD - t_new)), constant_values=NEG_INF)
    new_kv = p3[:, :, D_MODEL:3 * D_MODEL]
    nc, n_phys = cache_c_kv.shape[:2]
    kvt = cache_c_kv.transpose(0, 1, 3, 4, 5, 2).reshape(nc, n_phys, 2 * D_MODEL, PAGE)
    lft = cache_c_logf.transpose(0, 1, 3, 2)
    o = fox_decode(page_table, p3[:, :, :D_MODEL],
                   jnp.pad(new_kv, ((0, 0), (0, NEW_PAD - t_new), (0, 0))), cn_col, bias_new, kvt, lft, u)
    return (o, new_kv.reshape(dn, t_new, 2, N_HEADS, HEAD_DIM),
            logf[0, :, :nf].reshape(dn, t_new, nf))


def kernel(x_prompt, x_sample, c_prompt, c_sample, state_a_conv, state_b_kv_w128, state_b_kv_w512, state_b_kv_w2048, cache_c_kv, cache_c_logf, page_table, state_d_conv, ada_w, ada_b, norm_g, final_g, ffn_w13, ffn_w2, a_w_in, a_b_in, a_w_dw, a_b_dw, a_ln_g, a_ln_b, a_w_out, a_b_out, b_w_qkv, b_w_out, rel_bias, c_w_in, c_b_f, c_w_out, d_w_in, d_w_conv, d_w_out):
    bn, s_len, _ = x_prompt.shape
    dn, t_new, _ = x_sample.shape
    assert t_new == DEC_T and s_len % 512 == 0
    n_p, n_s = bn * s_len, dn * t_new
    depth = ada_w.shape[0]
    tm_p, tm_s = 1024, n_s
    tm_ffn = 512
    b_states = (state_b_kv_w128, state_b_kv_w512, state_b_kv_w2048)
    w13_all, w2_all = ffn_w13.astype(BF16), ffn_w2.astype(BF16)

    mod = ada_mod(jnp.concatenate([c_prompt, c_sample], axis=0), ada_w, ada_b)
    mod = mod.reshape(depth, bn + dn, 9, D_MODEL)

    xp = x_prompt.reshape(n_p, D_MODEL)
    xs = x_sample.reshape(n_s, D_MODEL)
    outs = {}
    for l in range(depth):
        kind, u = l % 4, l // 4
        mod_p = mod[l, :bn].reshape(bn, 9, 1, D_MODEL)
        mod_s = jnp.repeat(mod[l, bn:], t_new, axis=0).transpose(1, 0, 2)[None]
        ffn_paths = lambda xp, xs: ((xp, mod_p, tm_ffn), (xs, mod_s, tm_s))

        xp, xs = [ffn(x, m, 0, norm_g[l, 0], w13_all, w2_all, l, 0, tm) for x, m, tm in ffn_paths(xp, xs)]
        paths = ((xp, mod_p, tm_p), (xs, mod_s, tm_s))
        wide_paths = ((xp, mod_p, 2 * tm_p), (xs, mod_s, tm_s))

        if kind == 0:
            w_in, w_out = a_w_in[u].astype(BF16), a_w_out[u].astype(BF16)
            tn = 512
            us = [nmm(x, m, 1, norm_g[l, 1], w_in, (0, D_MODEL // tn), D_MODEL // tn, tn, tm,
                      bias=a_b_in[u], mode="glu") for x, m, tm in paths]
            up3, us3 = us[0].reshape(bn, s_len, D_MODEL), us[1].reshape(dn, t_new, D_MODEL)
            extras = (a_b_dw[u], a_ln_g[u], a_ln_b[u])
            yp = conv_mix("A", up3, up3, a_w_dw[u], 256, True, extras)
            halo_s = jnp.pad(state_a_conv[u], ((0, 0), (HALO - (CONV_A - 1), 0), (0, 0)))
            ys = conv_mix("A", us3, halo_s, a_w_dw[u], t_new, False, extras)
            xp = mmr([yp.reshape(n_p, D_MODEL)], xp, mod_p, 1, w_out, tm_p, bias=a_b_out[u])
            xs = mmr([ys.reshape(n_s, D_MODEL)], xs, mod_s, 1, w_out, tm_s, bias=a_b_out[u])
            outs.setdefault("a_p", []).append(up3[:, s_len - (CONV_A - 1):])
            outs.setdefault("a_s", []).append(
                jnp.concatenate([state_a_conv[u], us3], axis=1)[:, t_new:])
        elif kind == 1:
            w_qkv, w_out = b_w_qkv[u].astype(BF16), b_w_out[u].astype(BF16)
            ncol = w_qkv.shape[1] // 1024
            qkv_p, qkv_s = [nmm(x, m, 1, norm_g[l, 1], w_qkv, (0,), ncol, 1024, tm)
                            for x, m, tm in wide_paths]
            po, pl_, pst = dilated_prompt_stage(qkv_p.reshape(bn, s_len, -1), rel_bias)
            so, sl_, sst = dilated_sample_stage(qkv_s.reshape(dn, t_new, -1),
                                                [b[u] for b in b_states], rel_bias)
            xp = mmr(po + pl_, xp, mod_p, 1, w_out, tm_p // 2)
            xs = mmr(so + sl_, xs, mod_s, 1, w_out, tm_s)
            for g in range(len(B_PAIRS)):
                outs.setdefault("b%d_p" % g, []).append(pst[g])
                outs.setdefault("b%d_s" % g, []).append(sst[g])
        elif kind == 2:
            nf = c_b_f.shape[-1]
            w_in = jnp.pad(c_w_in[u], ((0, 0), (0, LANES - nf))).astype(BF16)
            w_out = c_w_out[u].astype(BF16)
            tn = 640
            ncol = w_in.shape[1] // tn
            pp, ps = [nmm(x, m, 1, norm_g[l, 1], w_in, (0,), ncol, tn, tm) for x, m, tm in wide_paths]
            o_p, kv_p, lf_p = fox_prompt_stage(pp.reshape(bn, s_len, -1), c_b_f[u])
            o_s, kv_s, lf_s = fox_sample_stage(ps.reshape(dn, t_new, -1), c_b_f[u], page_table,
                                               cache_c_kv, cache_c_logf, u)
            xp = mmr([o_p.reshape(n_p, D_MODEL)], xp, mod_p, 1, w_out, tm_p)
            xs = mmr([o_s], xs, mod_s, 1, w_out, tm_s)
            outs.setdefault("ckv_p", []).append(kv_p)
            outs.setdefault("clf_p", []).append(lf_p)
            outs.setdefault("ckv_s", []).append(kv_s)
            outs.setdefault("clf_s", []).append(lf_s)
        else:
            w_in, w_out = d_w_in[u].astype(BF16), d_w_out[u].astype(BF16)
            tn = 512
            nc = D_MODEL // tn
            (bg_p, z_p), (bg_s, z_s) = [
                nmm(x, m, 1, norm_g[l, 1], w_in, (0, nc, 2 * nc), nc, tn, tm, mode="gate3")
                for x, m, tm in paths]
            zp3, zs3 = z_p.reshape(bn, s_len, D_MODEL), z_s.reshape(dn, t_new, D_MODEL)
            yp = conv_mix("D", zp3, zp3, d_w_conv[u], 256, True, (bg_p.reshape(bn, s_len, D_MODEL),))
            halo_s = jnp.pad(state_d_conv[u], ((0, 0), (HALO - (CONV_D - 1), 0), (0, 0)))
            ys = conv_mix("D", zs3, halo_s, d_w_conv[u], t_new, False, (bg_s.reshape(dn, t_new, D_MODEL),))
            xp = mmr([yp.reshape(n_p, D_MODEL)], xp, mod_p, 1, w_out, tm_p)
            xs = mmr([ys.reshape(n_s, D_MODEL)], xs, mod_s, 1, w_out, tm_s)
            outs.setdefault("d_p", []).append(zp3[:, s_len - (CONV_D - 1):])
            outs.setdefault("d_s", []).append(
                jnp.concatenate([state_d_conv[u], zs3], axis=1)[:, t_new:])

        fg = final_g if l == depth - 1 else None
        xp, xs = [ffn(x, m, 2, norm_g[l, 2], w13_all, w2_all, l, 1, tm, final_g=fg)
                  for x, m, tm in ffn_paths(xp, xs)]

    st = lambda key: jnp.stack(outs[key])
    return (xp.reshape(bn, s_len, D_MODEL), xs.reshape(dn, t_new, D_MODEL),
            st("a_p"), st("a_s"), st("b0_p"), st("b0_s"), st("b1_p"), st("b1_s"), st("b2_p"), st("b2_s"),
            st("ckv_p"), st("clf_p"), st("ckv_s"), st("clf_s"), st("d_p"), st("d_s"))
```
